```python
import jax, jax.numpy as jnp
from jax import lax
import numpy as np

D_MODEL = 2048
BATCH = 1
SEQ = 8192
DEPTH = 4
DEC_BATCH = 16
DEC_SEQ = 16
PAST_LEN = 2048

CHUNK = 64
N_A = DEPTH // 2
N_B = DEPTH - N_A
CONV_WIDTH = 31
CONV_PAD = CONV_WIDTH - 1
N_HEADS = 32
N_KV_HEADS = 4
HEAD_DIM = 64
GROUP = N_HEADS // N_KV_HEADS
WINDOW = 128
WIN_CHUNKS = WINDOW // CHUNK
D_FF = ((-(-8 * D_MODEL // 3) + 255) // 256) * 256
ROPE_THETA = 10000.0
EPS = 1e-6
SCALE = HEAD_DIM ** -0.5

kernel_name = 'streaming_conformer_conv_yoco_swa_sink'


def rms_norm(x, g):
    xf = x.astype(jnp.float32)
    y = xf * lax.rsqrt(jnp.mean(xf * xf, axis=-1, keepdims=True) + EPS)
    return (y * g.astype(jnp.float32)).astype(x.dtype)


def layer_norm(x, g, b):
    xf = x.astype(jnp.float32)
    xc = xf - jnp.mean(xf, axis=-1, keepdims=True)
    var = jnp.mean(xc * xc, axis=-1, keepdims=True)
    y = xc * lax.rsqrt(var + EPS) * g.astype(jnp.float32) + b.astype(jnp.float32)
    return y.astype(x.dtype)


def rope(x, pos):
    half = HEAD_DIM // 2
    inv_freq = ROPE_THETA ** (-jnp.arange(half, dtype=jnp.float32) / half)
    ang = pos.astype(jnp.float32)[:, None] * inv_freq[None, :]
    cos = jnp.cos(ang)[:, None, :]
    sin = jnp.sin(ang)[:, None, :]
    xf = x.astype(jnp.float32)
    x1, x2 = xf[..., :half], xf[..., half:]
    return jnp.concatenate([x1 * cos - x2 * sin, x2 * cos + x1 * sin], axis=-1).astype(x.dtype)


def heads(h, w, g, n, pos):
    b, t = h.shape[:2]
    z = jnp.matmul(h, w).reshape(b, t, n, HEAD_DIM)
    return rope(rms_norm(z, g), pos)


def shared_kv(s, kv_norm, w_k, w_v, k_norm, pos):
    h = rms_norm(s, kv_norm)
    b, t = s.shape[:2]
    k = heads(h, w_k, k_norm, N_KV_HEADS, pos)
    v = jnp.matmul(h, w_v).reshape(b, t, N_KV_HEADS, HEAD_DIM)
    return k, v


def swiglu(x, g, w_gate, w_up, w_down):
    h = rms_norm(x, g)
    return jnp.matmul(jax.nn.silu(jnp.matmul(h, w_gate)) * jnp.matmul(h, w_up), w_down)


def conv_module(x, ctx, norm_g, w_pw1, b_pw1, w_dw, b_dw, ln_g, ln_b, w_pw2, b_pw2):
    h = rms_norm(x, norm_g)
    a = jnp.matmul(h, w_pw1) + b_pw1
    u = a[..., :D_MODEL] * jax.nn.sigmoid(a[..., D_MODEL:])
    up = jnp.concatenate([ctx.astype(u.dtype), u], axis=1)
    c = lax.conv_general_dilated(up, w_dw[:, None, :], window_strides=(1,), padding='VALID',
                                 dimension_numbers=('NWC', 'WIO', 'NWC'),
                                 feature_group_count=D_MODEL) + b_dw
    c = jax.nn.silu(layer_norm(c, ln_g, ln_b))
    return jnp.matmul(c, w_pw2) + b_pw2, up[:, -CONV_PAD:]


def sink_weights(s, sink):
    m = jnp.maximum(jnp.max(s, axis=-1, keepdims=True), sink)
    p = jnp.exp(s - m)
    return p / (jnp.sum(p, axis=-1, keepdims=True) + jnp.exp(sink - m))


def band_attention(q, k, v, sinks):
    b, t = q.shape[:2]
    nc = t // CHUNK
    lead = WIN_CHUNKS * CHUNK
    qb = q.reshape(b, nc, CHUNK, N_KV_HEADS, GROUP, HEAD_DIM)
    pad = ((0, 0), (lead, 0), (0, 0), (0, 0))
    kp = jnp.pad(k, pad).reshape(b, nc + WIN_CHUNKS, CHUNK, N_KV_HEADS, HEAD_DIM)
    vp = jnp.pad(v, pad).reshape(b, nc + WIN_CHUNKS, CHUNK, N_KV_HEADS, HEAD_DIM)
    kb = jnp.concatenate([kp[:, j:j + nc] for j in range(WIN_CHUNKS + 1)], axis=2)
    vb = jnp.concatenate([vp[:, j:j + nc] for j in range(WIN_CHUNKS + 1)], axis=2)
    key_chunk = jnp.arange(nc)[:, None] + jnp.repeat(jnp.arange(WIN_CHUNKS + 1) - WIN_CHUNKS, CHUNK)[None, :]
    valid = key_chunk >= 0
    s = jnp.einsum('bcqkgd,bcskd->bckgqs', qb, kb, preferred_element_type=jnp.float32) * SCALE
    s = jnp.where(valid[None, :, None, None, None, :], s, -jnp.inf)
    sink = sinks.astype(jnp.float32).reshape(N_KV_HEADS, GROUP)[None, None, :, :, None, None]
    p = sink_weights(s, sink).astype(v.dtype)
    o = jnp.einsum('bckgqs,bcskd->bcqkgd', p, vb)
    return o.reshape(b, t, N_HEADS * HEAD_DIM)


def window_attention_step(q, k_all, v_all, sinks):
    b, t = q.shape[:2]
    qg = q.reshape(b, t, N_KV_HEADS, GROUP, HEAD_DIM)
    s = jnp.einsum('btkgd,bskd->bkgts', qg, k_all, preferred_element_type=jnp.float32) * SCALE
    sink = sinks.astype(jnp.float32).reshape(N_KV_HEADS, GROUP)[None, :, :, None, None]
    p = sink_weights(s, sink).astype(v_all.dtype)
    o = jnp.einsum('bkgts,bskd->btkgd', p, v_all)
    return o.reshape(b, t, N_HEADS * HEAD_DIM)


def setup_inputs(seed: int = 0) -> dict:
    key = jax.random.key(seed)
    ks = iter(jax.random.split(key, 32))

    def nrm(shape, scale=1.0):
        return jax.random.normal(next(ks), shape, jnp.float32) * scale

    def gain(shape):
        return 1.0 + 0.01 * nrm(shape)

    D, F = D_MODEL, D_FF
    HQ, HKV = N_HEADS * HEAD_DIM, N_KV_HEADS * HEAD_DIM
    rows = min(WINDOW, PAST_LEN)
    return {
        'x_prompt': nrm((BATCH, SEQ, D)),
        'x_sample': nrm((DEC_BATCH, DEC_SEQ, D)),
        'state_conv': nrm((N_A, DEC_BATCH, CONV_PAD, D), 0.5),
        'cache_k': nrm((DEC_BATCH, rows, N_KV_HEADS, HEAD_DIM)),
        'cache_v': nrm((DEC_BATCH, rows, N_KV_HEADS, HEAD_DIM)),
        'ffn_norm': gain((DEPTH, D)),
        'w_gate': nrm((DEPTH, D, F), D ** -0.5),
        'w_up': nrm((DEPTH, D, F), D ** -0.5),
        'w_down': nrm((DEPTH, F, D), F ** -0.5),
        'conv_norm': gain((N_A, D)),
        'w_pw1': nrm((N_A, D, 2 * D), D ** -0.5),
        'b_pw1': nrm((N_A, 2 * D), 0.01),
        'w_dw': nrm((N_A, CONV_WIDTH, D), CONV_WIDTH ** -0.5),
        'b_dw': nrm((N_A, D), 0.01),
        'conv_ln_g': gain((N_A, D)),
        'conv_ln_b': nrm((N_A, D), 0.01),
        'w_pw2': nrm((N_A, D, D), D ** -0.5),
        'b_pw2': nrm((N_A, D), 0.01),
        'kv_norm': gain((D,)),
        'w_k': nrm((D, HKV), D ** -0.5),
        'w_v': nrm((D, HKV), D ** -0.5),
        'k_norm': gain((HEAD_DIM,)),
        'attn_norm': gain((N_B, D)),
        'w_q': nrm((N_B, D, HQ), D ** -0.5),
        'q_norm': gain((N_B, HEAD_DIM)),
        'sinks': nrm((N_B, N_HEADS), 0.5),
        'w_o': nrm((N_B, HQ, D), HQ ** -0.5),
    }


def reference(x_prompt, x_sample, state_conv, cache_k, cache_v,
              ffn_norm, w_gate, w_up, w_down,
              conv_norm, w_pw1, b_pw1, w_dw, b_dw, conv_ln_g, conv_ln_b, w_pw2, b_pw2,
              kv_norm, w_k, w_v, k_norm,
              attn_norm, w_q, q_norm, sinks, w_o):
    pos_p = jnp.arange(x_prompt.shape[1])
    pos_s = PAST_LEN + jnp.arange(x_sample.shape[1])
    xp, xs = x_prompt, x_sample
    ctx_p = jnp.zeros((xp.shape[0], CONV_PAD, D_MODEL), xp.dtype)
    conv_p, conv_s = [], []
    for i in range(DEPTH):
        if i < N_A:
            cargs = (conv_norm[i], w_pw1[i], b_pw1[i], w_dw[i], b_dw[i],
                     conv_ln_g[i], conv_ln_b[i], w_pw2[i], b_pw2[i])
            out_p, st_p = conv_module(xp, ctx_p, *cargs)
            out_s, st_s = conv_module(xs, state_conv[i], *cargs)
            conv_p.append(st_p)
            conv_s.append(st_s)
        else:
            j = i - N_A
            qp = heads(rms_norm(xp, attn_norm[j]), w_q[j], q_norm[j], N_HEADS, pos_p)
            qs = heads(rms_norm(xs, attn_norm[j]), w_q[j], q_norm[j], N_HEADS, pos_s)
            out_p = jnp.matmul(band_attention(qp, k_p, v_p, sinks[j]), w_o[j])
            out_s = jnp.matmul(window_attention_step(qs, k_all, v_all, sinks[j]), w_o[j])
        xp = xp + out_p
        xs = xs + out_s
        xp = xp + swiglu(xp, ffn_norm[i], w_gate[i], w_up[i], w_down[i])
        xs = xs + swiglu(xs, ffn_norm[i], w_gate[i], w_up[i], w_down[i])
        if i == N_A - 1:
            k_p, v_p = shared_kv(xp, kv_norm, w_k, w_v, k_norm, pos_p)
            k_s, v_s = shared_kv(xs, kv_norm, w_k, w_v, k_norm, pos_s)
            k_all = jnp.concatenate([cache_k.astype(k_s.dtype), k_s], axis=1)
            v_all = jnp.concatenate([cache_v.astype(v_s.dtype), v_s], axis=1)
    rows = cache_k.shape[1]
    return (xp, xs, jnp.stack(conv_p), k_p[:, -WINDOW:], v_p[:, -WINDOW:],
            jnp.stack(conv_s), k_all[:, -rows:], v_all[:, -rows:])
```

```python
from functools import partial

import jax
import jax.numpy as jnp
from jax import lax
from jax.experimental import pallas as pl
from jax.experimental.pallas import tpu as pltpu

D_MODEL = 2048
SEQ = 8192
DEPTH = 4
DEC_BATCH = 16
DEC_SEQ = 16
PAST_LEN = 2048
CHUNK = 64
N_A = DEPTH // 2
CONV_WIDTH = 31
CONV_PAD = CONV_WIDTH - 1
N_HEADS = 32
N_KV_HEADS = 4
HEAD_DIM = 64
GROUP = N_HEADS // N_KV_HEADS
WINDOW = 128
D_FF = 5632
ROPE_THETA = 10000.0
EPS = 1e-6
SCALE = HEAD_DIM ** -0.5

M_SAMPLE = DEC_BATCH * DEC_SEQ
M_TOTAL = SEQ + M_SAMPLE

LANES = 128
HALO = 32
VMEM_LIMIT = 56 * 1024 * 1024

BF16 = jnp.bfloat16
F32 = jnp.float32


def _params(semantics):
    return pltpu.CompilerParams(dimension_semantics=semantics, vmem_limit_bytes=VMEM_LIMIT)


def _rms_bf16(x, g):
    ms = jnp.mean(x * x, axis=-1, keepdims=True)
    return ((x * lax.rsqrt(ms + EPS)) * g).astype(BF16)


def _dot(a, b):
    return jnp.dot(a, b, preferred_element_type=F32)


def _silu(x):
    return x * jax.nn.sigmoid(x)


def _pw1_glu_kernel(x_ref, g_ref, wa_ref, wb_ref, ba_ref, bb_ref, u_ref, h_ref):
    @pl.when(pl.program_id(1) == 0)
    def _():
        h_ref[...] = _rms_bf16(x_ref[...], g_ref[...])

    h = h_ref[...]
    a = _dot(h, wa_ref[...]) + ba_ref[...]
    b = _dot(h, wb_ref[...]) + bb_ref[...]
    u_ref[...] = a * jax.nn.sigmoid(b)


def pw1_glu(x, g, w, b, *, tm=768, tn=512):
    m, d = x.shape
    nj = d // tn
    return pl.pallas_call(
        _pw1_glu_kernel,
        grid=(m // tm, nj),
        in_specs=[
            pl.BlockSpec((tm, d), lambda i, j: (i, 0)),
            pl.BlockSpec((1, d), lambda i, j: (0, 0)),
            pl.BlockSpec((d, tn), lambda i, j: (0, j)),
            pl.BlockSpec((d, tn), lambda i, j: (0, j + nj)),
            pl.BlockSpec((1, tn), lambda i, j: (0, j)),
            pl.BlockSpec((1, tn), lambda i, j: (0, j + nj)),
        ],
        out_specs=pl.BlockSpec((tm, tn), lambda i, j: (i, j)),
        out_shape=jax.ShapeDtypeStruct((m, d), F32),
        scratch_shapes=[pltpu.VMEM((tm, d), BF16)],
        compiler_params=_params(("parallel", "arbitrary")),
        name="pw1_glu",
    )(x, g, w, w, b, b)


CONV_ROWS = 32
CONV_LANES = 512


def _dw_ln_silu(win_ref, off, rows, wdw_ref, bdw_ref, lng_ref, lnb_ref, c_ref):
    d = c_ref.shape[1]
    rs = min(CONV_ROWS, rows)
    for r0 in range(0, rows, rs):
        for c0 in range(0, d, CONV_LANES):
            cs = slice(c0, c0 + CONV_LANES)
            acc = win_ref[pl.ds(off + r0, rs), cs] * wdw_ref[0:1, cs]
            for j in range(1, CONV_WIDTH):
                acc = acc + win_ref[pl.ds(off + r0 + j, rs), cs] * wdw_ref[j:j + 1, cs]
            c_ref[r0:r0 + rs, cs] = acc + bdw_ref[:, cs]
    c = c_ref[...]
    xc = c - jnp.mean(c, axis=-1, keepdims=True)
    var = jnp.mean(xc * xc, axis=-1, keepdims=True)
    y = xc * lax.rsqrt(var + EPS) * lng_ref[...] + lnb_ref[...]
    return _silu(y).astype(BF16)


def _conv_prompt_kernel(halo_ref, u_ref, wdw_ref, bdw_ref, lng_ref, lnb_ref, o_ref, win_ref, c_ref, *, tm):
    first = pl.program_id(0) == 0
    halo = halo_ref[...]
    win_ref[0:HALO, :] = jnp.where(first, jnp.zeros_like(halo), halo)
    win_ref[HALO:HALO + tm, :] = u_ref[...]
    o_ref[...] = _dw_ln_silu(win_ref, HALO - CONV_PAD, tm, wdw_ref, bdw_ref, lng_ref, lnb_ref, c_ref)


def conv_prompt(u, wdw, bdw, lng, lnb, *, tm=128):
    d = u.shape[1]
    vec = pl.BlockSpec((1, d), lambda i: (0, 0))
    return pl.pallas_call(
        partial(_conv_prompt_kernel, tm=tm),
        grid=(SEQ // tm,),
        in_specs=[
            pl.BlockSpec((HALO, d), lambda i: (jnp.maximum(i * (tm // HALO) - 1, 0), 0)),
            pl.BlockSpec((tm, d), lambda i: (i, 0)),
            pl.BlockSpec((CONV_WIDTH, d), lambda i: (0, 0)),
            vec, vec, vec,
        ],
        out_specs=pl.BlockSpec((tm, d), lambda i: (i, 0)),
        out_shape=jax.ShapeDtypeStruct((SEQ, d), BF16),
        scratch_shapes=[pltpu.VMEM((HALO + tm, d), F32), pltpu.VMEM((tm, d), F32)],
        compiler_params=_params(("parallel",)),
        name="conv_prompt",
    )(u, u, wdw, bdw, lng, lnb)


def _conv_sample_kernel(up_ref, wdw_ref, bdw_ref, lng_ref, lnb_ref, o_ref, c_ref):
    o_ref[0] = _dw_ln_silu(up_ref.at[0], HALO - CONV_PAD, DEC_SEQ, wdw_ref, bdw_ref, lng_ref, lnb_ref, c_ref)


def conv_sample(up, wdw, bdw, lng, lnb):
    b, r, d = up.shape
    vec = pl.BlockSpec((1, d), lambda i: (0, 0))
    return pl.pallas_call(
        _conv_sample_kernel,
        grid=(b,),
        in_specs=[
            pl.BlockSpec((1, r, d), lambda i: (i, 0, 0)),
            pl.BlockSpec((CONV_WIDTH, d), lambda i: (0, 0)),
            vec, vec, vec,
        ],
        out_specs=pl.BlockSpec((1, DEC_SEQ, d), lambda i: (i, 0, 0)),
        out_shape=jax.ShapeDtypeStruct((b, DEC_SEQ, d), BF16),
        scratch_shapes=[pltpu.VMEM((DEC_SEQ, d), F32)],
        compiler_params=_params(("parallel",)),
        name="conv_sample",
    )(up, wdw, bdw, lng, lnb)


def _proj_res_kernel(a_ref, w_ref, b_ref, x_ref, o_ref):
    o_ref[...] = x_ref[...] + (_dot(a_ref[...], w_ref[...]) + b_ref[...])


def proj_res(a, w, b, x, *, tm=768, tn=512):
    m, k = a.shape
    n = w.shape[1]
    return pl.pallas_call(
        _proj_res_kernel,
        grid=(m // tm, n // tn),
        in_specs=[
            pl.BlockSpec((tm, k), lambda i, j: (i, 0)),
            pl.BlockSpec((k, tn), lambda i, j: (0, j)),
            pl.BlockSpec((1, tn), lambda i, j: (0, j)),
            pl.BlockSpec((tm, tn), lambda i, j: (i, j)),
        ],
        out_specs=pl.BlockSpec((tm, tn), lambda i, j: (i, j)),
        out_shape=jax.ShapeDtypeStruct((m, n), F32),
        compiler_params=_params(("parallel", "parallel")),
        name="proj_res",
    )(a, w, b, x)


def _ffn_kernel(x_ref, g_ref, wg_ref, wu_ref, wd_ref, o_ref, h_ref):
    @pl.when(pl.program_id(1) == 0)
    def _():
        x = x_ref[...]
        h_ref[...] = _rms_bf16(x, g_ref[...])
        o_ref[...] = x

    h = h_ref[...]
    a = (_silu(_dot(h, wg_ref[...])) * _dot(h, wu_ref[...])).astype(BF16)
    o_ref[...] += _dot(a, wd_ref[...])


def ffn(x, g, wg, wu, wd, *, tm=768, tf=512):
    m, d = x.shape
    f = wg.shape[1]
    return pl.pallas_call(
        _ffn_kernel,
        grid=(m // tm, f // tf),
        in_specs=[
            pl.BlockSpec((tm, d), lambda i, j: (i, 0)),
            pl.BlockSpec((1, d), lambda i, j: (0, 0)),
            pl.BlockSpec((d, tf), lambda i, j: (0, j)),
            pl.BlockSpec((d, tf), lambda i, j: (0, j)),
            pl.BlockSpec((tf, d), lambda i, j: (j, 0)),
        ],
        out_specs=pl.BlockSpec((tm, d), lambda i, j: (i, 0)),
        out_shape=jax.ShapeDtypeStruct((m, d), F32),
        scratch_shapes=[pltpu.VMEM((tm, d), BF16)],
        compiler_params=_params(("parallel", "arbitrary")),
        name="ffn",
    )(x, g, wg, wu, wd)


def _head_inv_rms(z, gsum_ref):
    sq = z * z
    hi = sq.astype(BF16)
    lo = (sq - hi.astype(F32)).astype(BF16)
    ss = _dot(hi, gsum_ref[...]) + _dot(lo, gsum_ref[...])
    return lax.rsqrt(ss * (1.0 / HEAD_DIM) + EPS)


def _expand_heads(r, gexp_ref):
    hi = r.astype(BF16)
    lo = (r - hi.astype(F32)).astype(BF16)
    return _dot(hi, gexp_ref[...]) + _dot(lo, gexp_ref[...])


def _rope_block(zb, cos, sin, first_half):
    partner = jnp.where(first_half, pltpu.roll(zb, LANES - HEAD_DIM // 2, 1), pltpu.roll(zb, HEAD_DIM // 2, 1))
    return zb * cos + partner * sin


def _rope(z, cos, sin):
    lane = lax.broadcasted_iota(jnp.int32, (z.shape[0], LANES), 1)
    first_half = (lane & (HEAD_DIM - 1)) < HEAD_DIM // 2
    blocks = [_rope_block(z[:, c:c + LANES], cos, sin, first_half) for c in range(0, z.shape[1], LANES)]
    return jnp.concatenate(blocks, axis=1)


KV_DUP = N_KV_HEADS * LANES
KV_PLAIN = N_KV_HEADS * HEAD_DIM


def _kv_kernel(x_ref, g_ref, w_ref, gsum_ref, gexp_ref, gk_ref, cos_ref, sin_ref,
               k2_ref, v2_ref, kf_ref, vf_ref):
    h = _rms_bf16(x_ref[...], g_ref[...])
    z = _dot(h, w_ref[...])
    zk2 = z[:, :KV_DUP]
    zk = z[:, 2 * KV_DUP:2 * KV_DUP + KV_PLAIN]
    cos = cos_ref[...]
    sin = sin_ref[...]
    r2 = _expand_heads(_head_inv_rms(zk2, gsum_ref), gexp_ref)
    k2_ref[...] = _rope(zk2 * r2 * gk_ref[...], cos, sin).astype(BF16)
    v2_ref[...] = z[:, KV_DUP:2 * KV_DUP].astype(BF16)
    r1 = _expand_heads(_head_inv_rms(zk, gsum_ref.at[0:KV_PLAIN]), gexp_ref.at[:, 0:KV_PLAIN])
    kf_ref[...] = _rope(zk * r1 * gk_ref[:, 0:KV_PLAIN], cos, sin)
    vf_ref[...] = z[:, 2 * KV_DUP + KV_PLAIN:]


def kv_proj(x, g, w_all, gsum, gexp, gk, cos, sin, *, tm=256):
    m, d = x.shape
    n = w_all.shape[1]
    row = lambda w: pl.BlockSpec((tm, w), lambda i: (i, 0))
    full = lambda a: pl.BlockSpec(a.shape, lambda i: (0, 0))
    return pl.pallas_call(
        _kv_kernel,
        grid=(m // tm,),
        in_specs=[row(d), full(g), full(w_all), full(gsum), full(gexp), full(gk), row(LANES), row(LANES)],
        out_specs=[row(KV_DUP), row(KV_DUP), row(KV_PLAIN), row(KV_PLAIN)],
        out_shape=[
            jax.ShapeDtypeStruct((m, KV_DUP), BF16),
            jax.ShapeDtypeStruct((m, KV_DUP), BF16),
            jax.ShapeDtypeStruct((m, KV_PLAIN), F32),
            jax.ShapeDtypeStruct((m, KV_PLAIN), F32),
        ],
        compiler_params=_params(("parallel",)),
        name="kv_proj",
    )(x, g, w_all, gsum, gexp, gk, cos, sin)


def _q_kernel(x_ref, g_ref, w_ref, gsum_ref, gq_ref, cos_ref, sin_ref, q_ref, r_ref):
    h = _rms_bf16(x_ref[...], g_ref[...])
    z = _dot(h, w_ref[...])
    r_ref[...] = _head_inv_rms(z, gsum_ref)
    q_ref[...] = (_rope(z * gq_ref[...], cos_ref[...], sin_ref[...]) * SCALE).astype(BF16)


def q_proj(x, g, w, gsum, gq, cos, sin, *, tm=256):
    m, d = x.shape
    n = w.shape[1]
    row = lambda w_: pl.BlockSpec((tm, w_), lambda i: (i, 0))
    full = lambda a: pl.BlockSpec(a.shape, lambda i: (0, 0))
    return pl.pallas_call(
        _q_kernel,
        grid=(m // tm,),
        in_specs=[row(d), full(g), full(w), full(gsum), full(gq), row(LANES), row(LANES)],
        out_specs=[row(n), row(LANES)],
        out_shape=[jax.ShapeDtypeStruct((m, n), BF16), jax.ShapeDtypeStruct((m, LANES), F32)],
        compiler_params=_params(("parallel",)),
        name="q_proj",
    )(x, g, w, gsum, gq, cos, sin)


def _attend_group(q_ref, r_ref, sink_ref, kh, kk, vv, bias, rows):
    lane = lax.broadcasted_iota(jnp.int32, (rows, LANES), 1)
    low = lane < HEAD_DIM
    qs, rc, sc = [], [], []
    for pair in range(GROUP // 2):
        c0 = (kh * (GROUP // 2) + pair) * LANES
        qp = q_ref[:, c0:c0 + LANES]
        qs.append(jnp.where(low, qp, jnp.zeros_like(qp)))
        qs.append(jnp.where(low, jnp.zeros_like(qp), qp))
        for half in range(2):
            head = kh * GROUP + 2 * pair + half
            rc.append(r_ref[:, head:head + 1])
            sc.append(jnp.full((rows, 1), sink_ref[head], F32))
    qs = jnp.concatenate(qs, axis=0)
    rcol = jnp.concatenate(rc, axis=0)
    sink = jnp.concatenate(sc, axis=0)
    s = lax.dot_general(qs, kk, (((1,), (1,)), ((), ())), preferred_element_type=F32)
    s = s * rcol + bias
    m = jnp.maximum(jnp.max(s, axis=-1, keepdims=True), sink)
    p = jnp.exp(s - m)
    denom = jnp.sum(p, axis=-1, keepdims=True) + jnp.exp(sink - m)
    o = _dot(p.astype(BF16), vv) / denom
    outs = []
    for pair in range(GROUP // 2):
        o0 = o[(2 * pair) * rows:(2 * pair + 1) * rows]
        o1 = o[(2 * pair + 1) * rows:(2 * pair + 2) * rows]
        outs.append(jnp.where(low, o0, o1))
    return jnp.concatenate(outs, axis=1).astype(BF16)


TQ = 2 * CHUNK


def _attn_prompt_kernel(sink_ref, q_ref, r_ref, kp_ref, kc_ref, vp_ref, vc_ref, o_ref):
    i = pl.program_id(0)
    row = lax.broadcasted_iota(jnp.int32, (GROUP * TQ, 2 * TQ), 0)
    col = lax.broadcasted_iota(jnp.int32, (GROUP * TQ, 2 * TQ), 1)
    q_chunk = (row & (TQ - 1)) >> 6
    k_chunk = col >> 6
    rel = k_chunk - 2 - q_chunk
    visible = (rel <= 0) & (rel >= -2) & ((i > 0) | (k_chunk >= 2))
    bias = jnp.where(visible, 0.0, -jnp.inf).astype(F32)
    for kh in range(N_KV_HEADS):
        sl = slice(kh * LANES, (kh + 1) * LANES)
        kk = jnp.concatenate([kp_ref[:, sl], kc_ref[:, sl]], axis=0)
        vv = jnp.concatenate([vp_ref[:, sl], vc_ref[:, sl]], axis=0)
        o_ref[:, kh * GROUP * HEAD_DIM:(kh + 1) * GROUP * HEAD_DIM] = _attend_group(
            q_ref, r_ref, sink_ref, kh, kk, vv, bias, TQ)


def attn_prompt(sinks, q, r, k2, v2):
    hq = q.shape[1]
    cur = lambda w: pl.BlockSpec((TQ, w), lambda i: (i, 0))
    prev = lambda w: pl.BlockSpec((TQ, w), lambda i: (jnp.maximum(i - 1, 0), 0))
    return pl.pallas_call(
        _attn_prompt_kernel,
        grid=(SEQ // TQ,),
        in_specs=[
            pl.BlockSpec(memory_space=pltpu.SMEM),
            cur(hq), cur(LANES), prev(KV_DUP), cur(KV_DUP), prev(KV_DUP), cur(KV_DUP),
        ],
        out_specs=cur(hq),
        out_shape=jax.ShapeDtypeStruct((SEQ, hq), BF16),
        compiler_params=_params(("parallel",)),
        name="attn_prompt",
    )(sinks, q, r, k2, k2, v2, v2)


KEYS_SAMPLE = 256


def _attn_sample_kernel(sink_ref, q_ref, r_ref, k_ref, v_ref, o_ref):
    col = lax.broadcasted_iota(jnp.int32, (GROUP * DEC_SEQ, KEYS_SAMPLE), 1)
    bias = jnp.where(col < WINDOW + DEC_SEQ, 0.0, -jnp.inf).astype(F32)
    for kh in range(N_KV_HEADS):
        sl = slice(kh * LANES, (kh + 1) * LANES)
        o_ref[:, kh * GROUP * HEAD_DIM:(kh + 1) * GROUP * HEAD_DIM] = _attend_group(
            q_ref, r_ref, sink_ref, kh, k_ref[0, :, sl], v_ref[0, :, sl], bias, DEC_SEQ)


def attn_sample(sinks, q, r, k2, v2):
    hq = q.shape[1]
    row = lambda w: pl.BlockSpec((DEC_SEQ, w), lambda b: (b, 0))
    keys = pl.BlockSpec((1, KEYS_SAMPLE, KV_DUP), lambda b: (b, 0, 0))
    return pl.pallas_call(
        _attn_sample_kernel,
        grid=(DEC_BATCH,),
        in_specs=[pl.BlockSpec(memory_space=pltpu.SMEM), row(hq), row(LANES), keys, keys],
        out_specs=row(hq),
        out_shape=jax.ShapeDtypeStruct((M_SAMPLE, hq), BF16),
        compiler_params=_params(("parallel",)),
        name="attn_sample",
    )(sinks, q, r, k2, v2)


def _rope_tables(pos):
    half = HEAD_DIM // 2
    inv_freq = ROPE_THETA ** (-jnp.arange(half, dtype=F32) / half)
    ang = pos.astype(F32)[:, None] * inv_freq[None, :]
    cos = jnp.cos(ang)
    sin = jnp.sin(ang)
    cos = jnp.tile(jnp.concatenate([cos, cos], axis=1), (1, LANES // HEAD_DIM))
    sin = jnp.tile(jnp.concatenate([-sin, sin], axis=1), (1, LANES // HEAD_DIM))
    return cos, sin


def _head_indicator(n):
    head = jnp.arange(n, dtype=jnp.int32)[:, None] // HEAD_DIM
    return (head == jnp.arange(LANES, dtype=jnp.int32)[None, :]).astype(BF16)


def _dup_heads(w):
    lead = w.shape[:-1]
    w4 = w.reshape(lead + (N_KV_HEADS, 1, HEAD_DIM))
    return jnp.broadcast_to(w4, lead + (N_KV_HEADS, 2, HEAD_DIM)).reshape(lead + (KV_DUP,))


def kernel(x_prompt, x_sample, state_conv, cache_k, cache_v, ffn_norm, w_gate, w_up, w_down,
           conv_norm, w_pw1, b_pw1, w_dw, b_dw, conv_ln_g, conv_ln_b, w_pw2, b_pw2,
           kv_norm, w_k, w_v, k_norm, attn_norm, w_q, q_norm, sinks, w_o):
    d = D_MODEL
    hq = N_HEADS * HEAD_DIM
    x = jnp.concatenate([x_prompt.reshape(SEQ, d), x_sample.reshape(M_SAMPLE, d)], axis=0)

    pos = jnp.concatenate([jnp.arange(SEQ), jnp.tile(PAST_LEN + jnp.arange(DEC_SEQ), DEC_BATCH)])
    cos, sin = _rope_tables(pos)
    gsum = _head_indicator(hq)
    gexp = gsum[:KV_DUP].T
    zero_bias = jnp.zeros((1, d), F32)

    conv_p, conv_s = [], []
    for i in range(DEPTH):
        if i < N_A:
            u = pw1_glu(x, conv_norm[i][None], w_pw1[i].astype(BF16), b_pw1[i][None])
            u_s = u[SEQ:].reshape(DEC_BATCH, DEC_SEQ, d)
            up_s = jnp.concatenate(
                [jnp.zeros((DEC_BATCH, HALO - CONV_PAD, d), F32), state_conv[i], u_s], axis=1)
            cargs = (w_dw[i], b_dw[i][None], conv_ln_g[i][None], conv_ln_b[i][None])
            cn = jnp.concatenate(
                [conv_prompt(u, *cargs), conv_sample(up_s, *cargs).reshape(M_SAMPLE, d)], axis=0)
            x = proj_res(cn, w_pw2[i].astype(BF16), b_pw2[i][None], x)
            conv_p.append(u[SEQ - CONV_PAD:SEQ][None])
            conv_s.append(up_s[:, -CONV_PAD:])
        else:
            j = i - N_A
            q, r = q_proj(x, attn_norm[j][None], w_q[j].astype(BF16), gsum,
                          jnp.tile(q_norm[j], N_HEADS)[None], cos, sin)
            o_p = attn_prompt(sinks[j], q, r, k2, v2)
            o_s = attn_sample(sinks[j], q[SEQ:], r[SEQ:], k2_s, v2_s)
            x = proj_res(jnp.concatenate([o_p, o_s], axis=0), w_o[j].astype(BF16), zero_bias, x)
        x = ffn(x, ffn_norm[i][None], w_gate[i].astype(BF16), w_up[i].astype(BF16),
                w_down[i].astype(BF16))
        if i == N_A - 1:
            w_all = jnp.concatenate([_dup_heads(w_k), _dup_heads(w_v), w_k, w_v], axis=1).astype(BF16)
            gk = jnp.tile(k_norm, KV_DUP // HEAD_DIM)[None]
            k2, v2, kf, vf = kv_proj(x, kv_norm[None], w_all, gsum[:KV_DUP], gexp, gk, cos, sin)
            pad = jnp.zeros((DEC_BATCH, KEYS_SAMPLE - WINDOW - DEC_SEQ, KV_DUP), BF16)
            k2_s = jnp.concatenate(
                [_dup_heads(cache_k.reshape(DEC_BATCH, WINDOW, KV_PLAIN)).astype(BF16),
                 k2[SEQ:].reshape(DEC_BATCH, DEC_SEQ, KV_DUP), pad], axis=1)
            v2_s = jnp.concatenate(
                [_dup_heads(cache_v.reshape(DEC_BATCH, WINDOW, KV_PLAIN)).astype(BF16),
                 v2[SEQ:].reshape(DEC_BATCH, DEC_SEQ, KV_DUP), pad], axis=1)

    kv4 = lambda a, b: a.reshape(b, -1, N_KV_HEADS, HEAD_DIM)
    k_p = kv4(kf[SEQ - WINDOW:SEQ], 1)
    v_p = kv4(vf[SEQ - WINDOW:SEQ], 1)
    k_s = jnp.concatenate([cache_k, kv4(kf[SEQ:], DEC_BATCH)], axis=1)[:, -WINDOW:]
    v_s = jnp.concatenate([cache_v, kv4(vf[SEQ:], DEC_BATCH)], axis=1)[:, -WINDOW:]
    return (x[:SEQ].reshape(1, SEQ, d), x[SEQ:].reshape(DEC_BATCH, DEC_SEQ, d),
            jnp.stack(conv_p), k_p, v_p, jnp.stack(conv_s), k_s, v_s)
```

```python
from functools import partial

import jax
import jax.numpy as jnp
from jax import lax
from jax.experimental import pallas as pl
from jax.experimental.pallas import tpu as pltpu

D_MODEL = 2048
SEQ = 8192
DEPTH = 4
DEC_BATCH = 16
DEC_SEQ = 16
PAST_LEN = 2048
CHUNK = 64
N_A = DEPTH // 2
CONV_WIDTH = 31
CONV_PAD = CONV_WIDTH - 1
N_HEADS = 32
N_KV_HEADS = 4
HEAD_DIM = 64
GROUP = N_HEADS // N_KV_HEADS
WINDOW = 128
D_FF = 5632
ROPE_THETA = 10000.0
EPS = 1e-6
SCALE = HEAD_DIM ** -0.5

M_SAMPLE = DEC_BATCH * DEC_SEQ
M_TOTAL = SEQ + M_SAMPLE

LANES = 128
SUBLANES = 8
HALO = 32
VMEM_LIMIT = 56 * 1024 * 1024

TM = 1056
N_TILES = M_TOTAL // TM
TAIL0 = SEQ - (N_TILES - 1) * TM
TAIL_BLOCK = SEQ // M_SAMPLE

BF16 = jnp.bfloat16
F32 = jnp.float32


def _params(semantics):
    return pltpu.CompilerParams(dimension_semantics=semantics, vmem_limit_bytes=VMEM_LIMIT)


def _rms_bf16(x, g):
    ms = jnp.mean(x * x, axis=-1, keepdims=True)
    return ((x * lax.rsqrt(ms + EPS)) * g).astype(BF16)


def _dot(a, b):
    return jnp.dot(a, b, preferred_element_type=F32)


def _silu(x):
    return x * jax.nn.sigmoid(x)


def _is_last_tile():
    return pl.program_id(0) == pl.num_programs(0) - 1


def _main_rows(width):
    return pl.BlockSpec((TM, width), lambda i, j: (i, 0))


def _pw1_glu_kernel(x_ref, xt_ref, g_ref, wa_ref, wb_ref, ba_ref, bb_ref, u_ref, h_ref):
    @pl.when(pl.program_id(1) == 0)
    def _():
        g = g_ref[...]
        h_ref[0:TAIL0] = _rms_bf16(x_ref[0:TAIL0], g)

        @pl.when(jnp.logical_not(_is_last_tile()))
        def _():
            h_ref[TAIL0:TM] = _rms_bf16(x_ref[TAIL0:TM], g)

        @pl.when(_is_last_tile())
        def _():
            h_ref[TAIL0:TM] = _rms_bf16(xt_ref[...], g)

    h = h_ref[...]
    a = _dot(h, wa_ref[...].astype(BF16)) + ba_ref[...]
    b = _dot(h, wb_ref[...].astype(BF16)) + bb_ref[...]
    u_ref[...] = a * jax.nn.sigmoid(b)


def pw1_glu(x_main, x_tail, tail_block, g, w, layer, b, *, tn=256):
    d = x_main.shape[1]
    nj = d // tn
    return pl.pallas_call(
        _pw1_glu_kernel,
        grid=(N_TILES, nj),
        in_specs=[
            _main_rows(d),
            pl.BlockSpec((M_SAMPLE, d), lambda i, j: (tail_block, 0)),
            pl.BlockSpec((1, d), lambda i, j: (0, 0)),
            pl.BlockSpec((None, d, tn), lambda i, j: (layer, 0, j)),
            pl.BlockSpec((None, d, tn), lambda i, j: (layer, 0, j + nj)),
            pl.BlockSpec((1, tn), lambda i, j: (0, j)),
            pl.BlockSpec((1, tn), lambda i, j: (0, j + nj)),
        ],
        out_specs=pl.BlockSpec((TM, tn), lambda i, j: (i, j)),
        out_shape=jax.ShapeDtypeStruct((M_TOTAL, d), F32),
        scratch_shapes=[pltpu.VMEM((TM, d), BF16)],
        compiler_params=_params(("parallel", "arbitrary")),
        name="pw1_glu",
    )(x_main, x_tail, g, w, w, b, b)


CONV_ROWS = 64
CONV_LANES = 256


def _dw_block(win_ref, off, r0, rs, cs, wdw_ref):
    acc = None
    for r in range(SUBLANES):
        es = [e for e in range(off, off + CONV_WIDTH) if e % SUBLANES == r]
        rows = rs + SUBLANES if r else rs
        a = None
        for e in es:
            j = e - off
            term = win_ref[pl.ds(r0 + e - r, rows), cs] * wdw_ref[j:j + 1, cs]
            a = term if a is None else a + term
        if r:
            a = pltpu.roll(a, rows - r, 0)[0:rs]
        acc = a if acc is None else acc + a
    return acc


def _dw_ln_silu(win_ref, off, rows, wdw_ref, bdw_ref, lng_ref, lnb_ref, c_ref):
    d = c_ref.shape[1]
    rs = min(CONV_ROWS, rows)
    for r0 in range(0, rows, rs):
        for c0 in range(0, d, CONV_LANES):
            cs = slice(c0, c0 + CONV_LANES)
            c_ref[r0:r0 + rs, cs] = _dw_block(win_ref, off, r0, rs, cs, wdw_ref) + bdw_ref[:, cs]
    c = c_ref[...]
    xc = c - jnp.mean(c, axis=-1, keepdims=True)
    var = jnp.mean(xc * xc, axis=-1, keepdims=True)
    y = xc * lax.rsqrt(var + EPS) * lng_ref[...] + lnb_ref[...]
    return _silu(y).astype(BF16)


def _conv_prompt_kernel(halo_ref, u_ref, wdw_ref, bdw_ref, lng_ref, lnb_ref, o_ref, win_ref, c_ref, *, tm):
    first = pl.program_id(0) == 0
    halo = halo_ref[...]
    win_ref[0:HALO, :] = jnp.where(first, jnp.zeros_like(halo), halo)
    win_ref[HALO:HALO + tm, :] = u_ref[...]
    o_ref[...] = _dw_ln_silu(win_ref, HALO - CONV_PAD, tm, wdw_ref, bdw_ref, lng_ref, lnb_ref, c_ref)


def conv_prompt(u, wdw, bdw, lng, lnb, *, tm=256):
    d = u.shape[1]
    vec = pl.BlockSpec((1, d), lambda i: (0, 0))
    return pl.pallas_call(
        partial(_conv_prompt_kernel, tm=tm),
        grid=(SEQ // tm,),
        in_specs=[
            pl.BlockSpec((HALO, d), lambda i: (jnp.maximum(i * (tm // HALO) - 1, 0), 0)),
            pl.BlockSpec((tm, d), lambda i: (i, 0)),
            pl.BlockSpec((CONV_WIDTH, d), lambda i: (0, 0)),
            vec, vec, vec,
        ],
        out_specs=pl.BlockSpec((tm, d), lambda i: (i, 0)),
        out_shape=jax.ShapeDtypeStruct((SEQ, d), BF16),
        scratch_shapes=[pltpu.VMEM((HALO + tm, d), F32), pltpu.VMEM((tm, d), F32)],
        compiler_params=_params(("parallel",)),
        name="conv_prompt",
    )(u, u, wdw, bdw, lng, lnb)


def _conv_sample_kernel(up_ref, wdw_ref, bdw_ref, lng_ref, lnb_ref, o_ref, c_ref):
    o_ref[0] = _dw_ln_silu(up_ref.at[0], HALO - CONV_PAD, DEC_SEQ, wdw_ref, bdw_ref, lng_ref, lnb_ref, c_ref)


def conv_sample(up, wdw, bdw, lng, lnb):
    b, r, d = up.shape
    vec = pl.BlockSpec((1, d), lambda i: (0, 0))
    return pl.pallas_call(
        _conv_sample_kernel,
        grid=(b,),
        in_specs=[
            pl.BlockSpec((1, r, d), lambda i: (i, 0, 0)),
            pl.BlockSpec((CONV_WIDTH, d), lambda i: (0, 0)),
            vec, vec, vec,
        ],
        out_specs=pl.BlockSpec((1, DEC_SEQ, d), lambda i: (i, 0, 0)),
        out_shape=jax.ShapeDtypeStruct((b, DEC_SEQ, d), BF16),
        scratch_shapes=[pltpu.VMEM((DEC_SEQ, d), F32)],
        compiler_params=_params(("parallel",)),
        name="conv_sample",
    )(up, wdw, bdw, lng, lnb)


def _proj_res_kernel(a_ref, at_ref, w_ref, b_ref, x_ref, xt_ref, o_ref):
    w = w_ref[...].astype(BF16)
    b = b_ref[...]

    @pl.when(jnp.logical_not(_is_last_tile()))
    def _():
        o_ref[...] = x_ref[...] + (_dot(a_ref[...], w) + b)

    @pl.when(_is_last_tile())
    def _():
        o_ref[0:TAIL0] = x_ref[0:TAIL0] + (_dot(a_ref[0:TAIL0], w) + b)
        o_ref[TAIL0:TM] = xt_ref[...] + (_dot(at_ref[...], w) + b)


def proj_res(a_main, a_tail, w, layer, b, x_main, x_tail, tail_block, *, tn=512):
    k = a_main.shape[1]
    n = w.shape[2]
    return pl.pallas_call(
        _proj_res_kernel,
        grid=(N_TILES, n // tn),
        in_specs=[
            _main_rows(k),
            pl.BlockSpec((M_SAMPLE, k), lambda i, j: (0, 0)),
            pl.BlockSpec((None, k, tn), lambda i, j: (layer, 0, j)),
            pl.BlockSpec((1, tn), lambda i, j: (0, j)),
            pl.BlockSpec((TM, tn), lambda i, j: (i, j)),
            pl.BlockSpec((M_SAMPLE, tn), lambda i, j: (tail_block, j)),
        ],
        out_specs=pl.BlockSpec((TM, tn), lambda i, j: (i, j)),
        out_shape=jax.ShapeDtypeStruct((M_TOTAL, n), F32),
        compiler_params=_params(("parallel", "parallel")),
        name="proj_res",
    )(a_main, a_tail, w, b, x_main, x_tail)


def _ffn_step(x_ref, g_ref, wg_ref, wu_ref, wd_ref, o_ref, h_ref):
    @pl.when(pl.program_id(1) == 0)
    def _():
        x = x_ref[...]
        h_ref[...] = _rms_bf16(x, g_ref[...])
        o_ref[...] = x

    h = h_ref[...]
    gate = _dot(h, wg_ref[...].astype(BF16))
    up = _dot(h, wu_ref[...].astype(BF16))
    a = (_silu(gate) * up).astype(BF16)
    o_ref[...] += _dot(a, wd_ref[...].astype(BF16))


def _ffn_kernel(x_ref, g_ref, wg_ref, wu_ref, wd_ref, o_ref, h_ref):
    _ffn_step(x_ref, g_ref, wg_ref, wu_ref, wd_ref, o_ref, h_ref)


def _ffn_split_kernel(x_ref, g_ref, wg_ref, wu_ref, wd_ref, o_ref, ot_ref, h_ref):
    _ffn_step(x_ref, g_ref, wg_ref, wu_ref, wd_ref, o_ref, h_ref)

    @pl.when(_is_last_tile() & (pl.program_id(1) == pl.num_programs(1) - 1))
    def _():
        ot_ref[...] = o_ref[TAIL0:TM]


def ffn(x, g, wg, wu, wd, layer, *, split_out=False, tf=256):
    m, d = x.shape
    f = wg.shape[2]
    acc = pl.BlockSpec((TM, d), lambda i, j: (i, 0), pipeline_mode=pl.Buffered(1))
    in_specs = [
        _main_rows(d),
        pl.BlockSpec((1, d), lambda i, j: (0, 0)),
        pl.BlockSpec((None, d, tf), lambda i, j: (layer, 0, j)),
        pl.BlockSpec((None, d, tf), lambda i, j: (layer, 0, j)),
        pl.BlockSpec((None, tf, d), lambda i, j: (layer, j, 0)),
    ]
    common = dict(grid=(N_TILES, f // tf), in_specs=in_specs, scratch_shapes=[pltpu.VMEM((TM, d), BF16)])
    if split_out:
        return pl.pallas_call(
            _ffn_split_kernel,
            out_specs=[acc, pl.BlockSpec((M_SAMPLE, d), lambda i, j: (0, 0))],
            out_shape=[jax.ShapeDtypeStruct((SEQ, d), F32), jax.ShapeDtypeStruct((M_SAMPLE, d), F32)],
            compiler_params=_params(("arbitrary", "arbitrary")),
            name="ffn_split",
            **common,
        )(x, g, wg, wu, wd)
    return pl.pallas_call(
        _ffn_kernel,
        out_specs=acc,
        out_shape=jax.ShapeDtypeStruct((m, d), F32),
        compiler_params=_params(("parallel", "arbitrary")),
        name="ffn",
        **common,
    )(x, g, wg, wu, wd)


def _head_inv_rms(z, gsum_ref):
    sq = z * z
    hi = sq.astype(BF16)
    lo = (sq - hi.astype(F32)).astype(BF16)
    ss = _dot(hi, gsum_ref[...]) + _dot(lo, gsum_ref[...])
    return lax.rsqrt(ss * (1.0 / HEAD_DIM) + EPS)


def _expand_heads(r, gexp_ref):
    hi = r.astype(BF16)
    lo = (r - hi.astype(F32)).astype(BF16)
    return _dot(hi, gexp_ref[...]) + _dot(lo, gexp_ref[...])


def _rope_block(zb, cos, sin, first_half):
    partner = jnp.where(first_half, pltpu.roll(zb, LANES - HEAD_DIM // 2, 1), pltpu.roll(zb, HEAD_DIM // 2, 1))
    return zb * cos + partner * sin


def _rope(z, cos, sin):
    lane = lax.broadcasted_iota(jnp.int32, (z.shape[0], LANES), 1)
    first_half = (lane & (HEAD_DIM - 1)) < HEAD_DIM // 2
    blocks = [_rope_block(z[:, c:c + LANES], cos, sin, first_half) for c in range(0, z.shape[1], LANES)]
    return jnp.concatenate(blocks, axis=1)


KV_DUP = N_KV_HEADS * LANES
KV_PLAIN = N_KV_HEADS * HEAD_DIM


def _kv_kernel(x_ref, g_ref, w_ref, gsum_ref, gexp_ref, gk_ref, cos_ref, sin_ref,
               k2_ref, v2_ref, kf_ref, vf_ref):
    h = _rms_bf16(x_ref[...], g_ref[...])
    z = _dot(h, w_ref[...])
    zk2 = z[:, :KV_DUP]
    zk = z[:, 2 * KV_DUP:2 * KV_DUP + KV_PLAIN]
    cos = cos_ref[...]
    sin = sin_ref[...]
    r2 = _expand_heads(_head_inv_rms(zk2, gsum_ref), gexp_ref)
    k2_ref[...] = _rope(zk2 * r2 * gk_ref[...], cos, sin).astype(BF16)
    v2_ref[...] = z[:, KV_DUP:2 * KV_DUP].astype(BF16)
    r1 = _expand_heads(_head_inv_rms(zk, gsum_ref.at[0:KV_PLAIN]), gexp_ref.at[:, 0:KV_PLAIN])
    kf_ref[...] = _rope(zk * r1 * gk_ref[:, 0:KV_PLAIN], cos, sin)
    vf_ref[...] = z[:, 2 * KV_DUP + KV_PLAIN:]


def kv_proj(x, g, w_all, gsum, gexp, gk, cos, sin, *, tm=256):
    m, d = x.shape
    row = lambda w: pl.BlockSpec((tm, w), lambda i: (i, 0))
    full = lambda a: pl.BlockSpec(a.shape, lambda i: (0, 0))
    return pl.pallas_call(
        _kv_kernel,
        grid=(m // tm,),
        in_specs=[row(d), full(g), full(w_all), full(gsum), full(gexp), full(gk), row(LANES), row(LANES)],
        out_specs=[row(KV_DUP), row(KV_DUP), row(KV_PLAIN), row(KV_PLAIN)],
        out_shape=[
            jax.ShapeDtypeStruct((m, KV_DUP), BF16),
            jax.ShapeDtypeStruct((m, KV_DUP), BF16),
            jax.ShapeDtypeStruct((m, KV_PLAIN), F32),
            jax.ShapeDtypeStruct((m, KV_PLAIN), F32),
        ],
        compiler_params=_params(("parallel",)),
        name="kv_proj",
    )(x, g, w_all, gsum, gexp, gk, cos, sin)


def _q_kernel(x_ref, g_ref, w_ref, gsum_ref, gq_ref, cos_ref, sin_ref, q_ref, r_ref, wb_ref):
    @pl.when(pl.program_id(0) == 0)
    def _():
        wb_ref[...] = w_ref[...].astype(BF16)

    h = _rms_bf16(x_ref[...], g_ref[...])
    z = _dot(h, wb_ref[...])
    r_ref[...] = _head_inv_rms(z, gsum_ref)
    q_ref[...] = (_rope(z * gq_ref[...], cos_ref[...], sin_ref[...]) * SCALE).astype(BF16)


def q_proj(x, g, w, layer, gsum, gq, cos, sin, *, tm=256):
    m, d = x.shape
    n = w.shape[2]
    row = lambda w_: pl.BlockSpec((tm, w_), lambda i: (i, 0))
    full = lambda a: pl.BlockSpec(a.shape, lambda i: (0, 0))
    return pl.pallas_call(
        _q_kernel,
        grid=(m // tm,),
        in_specs=[
            row(d), full(g),
            pl.BlockSpec((None, d, n), lambda i: (layer, 0, 0), pipeline_mode=pl.Buffered(1)),
            full(gsum), full(gq), row(LANES), row(LANES),
        ],
        out_specs=[row(n), row(LANES)],
        out_shape=[jax.ShapeDtypeStruct((m, n), BF16), jax.ShapeDtypeStruct((m, LANES), F32)],
        scratch_shapes=[pltpu.VMEM((d, n), BF16)],
        compiler_params=_params(("arbitrary",)),
        name="q_proj",
    )(x, g, w, gsum, gq, cos, sin)


def _attend_group(q_ref, r_ref, sink_ref, kh, kk, vv, bias, rows):
    lane = lax.broadcasted_iota(jnp.int32, (rows, LANES), 1)
    low = lane < HEAD_DIM
    qs, rc, sc = [], [], []
    for pair in range(GROUP // 2):
        c0 = (kh * (GROUP // 2) + pair) * LANES
        qp = q_ref[:, c0:c0 + LANES]
        qs.append(jnp.where(low, qp, jnp.zeros_like(qp)))
        qs.append(jnp.where(low, jnp.zeros_like(qp), qp))
        for half in range(2):
            head = kh * GROUP + 2 * pair + half
            rc.append(r_ref[:, head:head + 1])
            sc.append(jnp.full((rows, 1), sink_ref[head], F32))
    qs = jnp.concatenate(qs, axis=0)
    rcol = jnp.concatenate(rc, axis=0)
    sink = jnp.concatenate(sc, axis=0)
    s = lax.dot_general(qs, kk, (((1,), (1,)), ((), ())), preferred_element_type=F32)
    s = s * rcol + bias
    m = jnp.maximum(jnp.max(s, axis=-1, keepdims=True), sink)
    p = jnp.exp(s - m)
    denom = jnp.sum(p, axis=-1, keepdims=True) + jnp.exp(sink - m)
    o = _dot(p.astype(BF16), vv) / denom
    outs = []
    for pair in range(GROUP // 2):
        o0 = o[(2 * pair) * rows:(2 * pair + 1) * rows]
        o1 = o[(2 * pair + 1) * rows:(2 * pair + 2) * rows]
        outs.append(jnp.where(low, o0, o1))
    return jnp.concatenate(outs, axis=1).astype(BF16)


TQ = 2 * CHUNK


def _attn_prompt_kernel(sink_ref, q_ref, r_ref, kp_ref, kc_ref, vp_ref, vc_ref, o_ref):
    i = pl.program_id(0)
    row = lax.broadcasted_iota(jnp.int32, (GROUP * TQ, 2 * TQ), 0)
    col = lax.broadcasted_iota(jnp.int32, (GROUP * TQ, 2 * TQ), 1)
    q_chunk = (row & (TQ - 1)) >> 6
    k_chunk = col >> 6
    rel = k_chunk - 2 - q_chunk
    visible = (rel <= 0) & (rel >= -2) & ((i > 0) | (k_chunk >= 2))
    bias = jnp.where(visible, 0.0, -jnp.inf).astype(F32)
    for kh in range(N_KV_HEADS):
        sl = slice(kh * LANES, (kh + 1) * LANES)
        kk = jnp.concatenate([kp_ref[:, sl], kc_ref[:, sl]], axis=0)
        vv = jnp.concatenate([vp_ref[:, sl], vc_ref[:, sl]], axis=0)
        o_ref[:, kh * GROUP * HEAD_DIM:(kh + 1) * GROUP * HEAD_DIM] = _attend_group(
            q_ref, r_ref, sink_ref, kh, kk, vv, bias, TQ)


def attn_prompt(sinks, q, r, k2, v2):
    hq = q.shape[1]
    cur = lambda w: pl.BlockSpec((TQ, w), lambda i: (i, 0))
    prev = lambda w: pl.BlockSpec((TQ, w), lambda i: (jnp.maximum(i - 1, 0), 0))
    return pl.pallas_call(
        _attn_prompt_kernel,
        grid=(SEQ // TQ,),
        in_specs=[
            pl.BlockSpec(memory_space=pltpu.SMEM),
            cur(hq), cur(LANES), prev(KV_DUP), cur(KV_DUP), prev(KV_DUP), cur(KV_DUP),
        ],
        out_specs=cur(hq),
        out_shape=jax.ShapeDtypeStruct((SEQ, hq), BF16),
        compiler_params=_params(("parallel",)),
        name="attn_prompt",
    )(sinks, q, r, k2, k2, v2, v2)


KEYS_SAMPLE = 256


def _attn_sample_kernel(sink_ref, q_ref, r_ref, k_ref, v_ref, o_ref):
    col = lax.broadcasted_iota(jnp.int32, (GROUP * DEC_SEQ, KEYS_SAMPLE), 1)
    bias = jnp.where(col < WINDOW + DEC_SEQ, 0.0, -jnp.inf).astype(F32)
    for kh in range(N_KV_HEADS):
        sl = slice(kh * LANES, (kh + 1) * LANES)
        o_ref[:, kh * GROUP * HEAD_DIM:(kh + 1) * GROUP * HEAD_DIM] = _attend_group(
            q_ref, r_ref, sink_ref, kh, k_ref[0, :, sl], v_ref[0, :, sl], bias, DEC_SEQ)


def attn_sample(sinks, q, r, k2, v2):
    hq = q.shape[1]
    row = lambda w: pl.BlockSpec((DEC_SEQ, w), lambda b: (SEQ // DEC_SEQ + b, 0))
    keys = pl.BlockSpec((1, KEYS_SAMPLE, KV_DUP), lambda b: (b, 0, 0))
    return pl.pallas_call(
        _attn_sample_kernel,
        grid=(DEC_BATCH,),
        in_specs=[pl.BlockSpec(memory_space=pltpu.SMEM), row(hq), row(LANES), keys, keys],
        out_specs=pl.BlockSpec((DEC_SEQ, hq), lambda b: (b, 0)),
        out_shape=jax.ShapeDtypeStruct((M_SAMPLE, hq), BF16),
        compiler_params=_params(("parallel",)),
        name="attn_sample",
    )(sinks, q, r, k2, v2)


def _rope_tables(pos):
    half = HEAD_DIM // 2
    inv_freq = ROPE_THETA ** (-jnp.arange(half, dtype=F32) / half)
    ang = pos.astype(F32)[:, None] * inv_freq[None, :]
    cos = jnp.cos(ang)
    sin = jnp.sin(ang)
    cos = jnp.tile(jnp.concatenate([cos, cos], axis=1), (1, LANES // HEAD_DIM))
    sin = jnp.tile(jnp.concatenate([-sin, sin], axis=1), (1, LANES // HEAD_DIM))
    return cos, sin


def _head_indicator(n):
    head = jnp.arange(n, dtype=jnp.int32)[:, None] // HEAD_DIM
    return (head == jnp.arange(LANES, dtype=jnp.int32)[None, :]).astype(BF16)


def _dup_heads(w):
    lead = w.shape[:-1]
    w4 = w.reshape(lead + (N_KV_HEADS, 1, HEAD_DIM))
    return jnp.broadcast_to(w4, lead + (N_KV_HEADS, 2, HEAD_DIM)).reshape(lead + (KV_DUP,))


def kernel(x_prompt, x_sample, state_conv, cache_k, cache_v, ffn_norm, w_gate, w_up, w_down,
           conv_norm, w_pw1, b_pw1, w_dw, b_dw, conv_ln_g, conv_ln_b, w_pw2, b_pw2,
           kv_norm, w_k, w_v, k_norm, attn_norm, w_q, q_norm, sinks, w_o):
    d = D_MODEL
    hq = N_HEADS * HEAD_DIM
    x = (x_prompt.reshape(SEQ, d), x_sample.reshape(M_SAMPLE, d), 0)

    pos = jnp.concatenate([jnp.arange(SEQ), jnp.tile(PAST_LEN + jnp.arange(DEC_SEQ), DEC_BATCH)])
    cos, sin = _rope_tables(pos)
    gsum = _head_indicator(hq)
    gexp = gsum[:KV_DUP].T
    zero_bias = jnp.zeros((1, d), F32)

    conv_p, conv_s = [], []
    for i in range(DEPTH):
        if i < N_A:
            u = pw1_glu(*x, conv_norm[i][None], w_pw1, i, b_pw1[i][None])
            u_s = u[SEQ:].reshape(DEC_BATCH, DEC_SEQ, d)
            up_s = jnp.concatenate(
                [jnp.zeros((DEC_BATCH, HALO - CONV_PAD, d), F32), state_conv[i], u_s], axis=1)
            cargs = (w_dw[i], b_dw[i][None], conv_ln_g[i][None], conv_ln_b[i][None])
            cn_p = conv_prompt(u, *cargs)
            cn_s = conv_sample(up_s, *cargs).reshape(M_SAMPLE, d)
            y = proj_res(cn_p, cn_s, w_pw2, i, b_pw2[i][None], *x)
            conv_p.append(u[SEQ - CONV_PAD:SEQ][None])
            conv_s.append(up_s[:, -CONV_PAD:])
        else:
            j = i - N_A
            xs = x[0]
            q, r = q_proj(xs, attn_norm[j][None], w_q, j, gsum,
                          jnp.tile(q_norm[j], N_HEADS)[None], cos, sin)
            o_p = attn_prompt(sinks[j], q, r, k2, v2)
            o_s = attn_sample(sinks[j], q, r, k2_s, v2_s)
            y = proj_res(o_p, o_s, w_o, j, zero_bias, *x)
        if i < DEPTH - 1:
            xs = ffn(y, ffn_norm[i][None], w_gate, w_up, w_down, i)
            x = (xs, xs, TAIL_BLOCK)
        else:
            y_p, y_s = ffn(y, ffn_norm[i][None], w_gate, w_up, w_down, i, split_out=True)
        if i == N_A - 1:
            w_all = jnp.concatenate([_dup_heads(w_k), _dup_heads(w_v), w_k, w_v], axis=1).astype(BF16)
            gk = jnp.tile(k_norm, KV_DUP // HEAD_DIM)[None]
            k2, v2, kf, vf = kv_proj(xs, kv_norm[None], w_all, gsum[:KV_DUP], gexp, gk, cos, sin)
            pad = jnp.zeros((DEC_BATCH, KEYS_SAMPLE - WINDOW - DEC_SEQ, KV_DUP), BF16)
            k2_s = jnp.concatenate(
                [_dup_heads(cache_k.reshape(DEC_BATCH, WINDOW, KV_PLAIN)).astype(BF16),
                 k2[SEQ:].reshape(DEC_BATCH, DEC_SEQ, KV_DUP), pad], axis=1)
            v2_s = jnp.concatenate(
                [_dup_heads(cache_v.reshape(DEC_BATCH, WINDOW, KV_PLAIN)).astype(BF16),
                 v2[SEQ:].reshape(DEC_BATCH, DEC_SEQ, KV_DUP), pad], axis=1)

    kv4 = lambda a, b: a.reshape(b, -1, N_KV_HEADS, HEAD_DIM)
    k_p = kv4(kf[SEQ - WINDOW:SEQ], 1)
    v_p = kv4(vf[SEQ - WINDOW:SEQ], 1)
    k_s = jnp.concatenate([cache_k, kv4(kf[SEQ:], DEC_BATCH)], axis=1)[:, -WINDOW:]
    v_s = jnp.concatenate([cache_v, kv4(vf[SEQ:], DEC_BATCH)], axis=1)[:, -WINDOW:]
    return (y_p.reshape(1, SEQ, d), y_s.reshape(DEC_BATCH, DEC_SEQ, d),
            jnp.stack(conv_p), k_p, v_p, jnp.stack(conv_s), k_s, v_s)
```

```python
from functools import partial

import jax
import jax.numpy as jnp
from jax import lax
from jax.experimental import pallas as pl
from jax.experimental.pallas import tpu as pltpu

D_MODEL = 2048
SEQ = 8192
DEPTH = 4
DEC_BATCH = 16
DEC_SEQ = 16
PAST_LEN = 2048
CHUNK = 64
N_A = DEPTH // 2
CONV_WIDTH = 31
CONV_PAD = CONV_WIDTH - 1
N_HEADS = 32
N_KV_HEADS = 4
HEAD_DIM = 64
GROUP = N_HEADS // N_KV_HEADS
WINDOW = 128
D_FF = 5632
ROPE_THETA = 10000.0
EPS = 1e-6
SCALE = HEAD_DIM ** -0.5

M_SAMPLE = DEC_BATCH * DEC_SEQ
M_TOTAL = SEQ + M_SAMPLE

LANES = 128
SUBLANES = 8
HALO = 32
VMEM_LIMIT = 56 * 1024 * 1024

TM = 1056
N_TILES = M_TOTAL // TM
TAIL0 = SEQ - (N_TILES - 1) * TM
TAIL_BLOCK = SEQ // M_SAMPLE

BF16 = jnp.bfloat16
F32 = jnp.float32


def _params(semantics):
    return pltpu.CompilerParams(dimension_semantics=semantics, vmem_limit_bytes=VMEM_LIMIT)


def _rms_bf16(x, g):
    ms = jnp.mean(x * x, axis=-1, keepdims=True)
    return ((x * lax.rsqrt(ms + EPS)) * g).astype(BF16)


def _dot(a, b):
    return jnp.dot(a, b, preferred_element_type=F32)


def _silu(x):
    return x * jax.nn.sigmoid(x)


def _is_last_tile():
    return pl.program_id(0) == pl.num_programs(0) - 1


def _main_rows(width):
    return pl.BlockSpec((TM, width), lambda i, j: (i, 0))


def _pw1_glu_kernel(x_ref, xt_ref, g_ref, wa_ref, wb_ref, ba_ref, bb_ref, u_ref, h_ref):
    @pl.when(pl.program_id(1) == 0)
    def _():
        g = g_ref[...]
        h_ref[0:TAIL0] = _rms_bf16(x_ref[0:TAIL0], g)

        @pl.when(jnp.logical_not(_is_last_tile()))
        def _():
            h_ref[TAIL0:TM] = _rms_bf16(x_ref[TAIL0:TM], g)

        @pl.when(_is_last_tile())
        def _():
            h_ref[TAIL0:TM] = _rms_bf16(xt_ref[...], g)

    h = h_ref[...]
    a = _dot(h, wa_ref[...].astype(BF16)) + ba_ref[...]
    b = _dot(h, wb_ref[...].astype(BF16)) + bb_ref[...]
    u_ref[...] = a * jax.nn.sigmoid(b)


def pw1_glu(x_main, x_tail, tail_block, g, w, layer, b, *, tn=256):
    d = x_main.shape[1]
    nj = d // tn
    return pl.pallas_call(
        _pw1_glu_kernel,
        grid=(N_TILES, nj),
        in_specs=[
            _main_rows(d),
            pl.BlockSpec((M_SAMPLE, d), lambda i, j: (tail_block, 0)),
            pl.BlockSpec((1, d), lambda i, j: (0, 0)),
            pl.BlockSpec((None, d, tn), lambda i, j: (layer, 0, j)),
            pl.BlockSpec((None, d, tn), lambda i, j: (layer, 0, j + nj)),
            pl.BlockSpec((1, tn), lambda i, j: (0, j)),
            pl.BlockSpec((1, tn), lambda i, j: (0, j + nj)),
        ],
        out_specs=pl.BlockSpec((TM, tn), lambda i, j: (i, j)),
        out_shape=jax.ShapeDtypeStruct((M_TOTAL, d), F32),
        scratch_shapes=[pltpu.VMEM((TM, d), BF16)],
        compiler_params=_params(("parallel", "arbitrary")),
        name="pw1_glu",
    )(x_main, x_tail, g, w, w, b, b)


CONV_ROWS = 64
CONV_LANES = 256


def _dw_block(win_ref, off, r0, rs, cs, wdw_ref):
    acc = None
    for r in range(SUBLANES):
        es = [e for e in range(off, off + CONV_WIDTH) if e % SUBLANES == r]
        rows = rs + SUBLANES if r else rs
        a = None
        for e in es:
            j = e - off
            term = win_ref[pl.ds(r0 + e - r, rows), cs] * wdw_ref[j:j + 1, cs]
            a = term if a is None else a + term
        if r:
            a = pltpu.roll(a, rows - r, 0)[0:rs]
        acc = a if acc is None else acc + a
    return acc


def _dw_ln_silu(win_ref, off, rows, wdw_ref, bdw_ref, lng_ref, lnb_ref, c_ref):
    d = c_ref.shape[1]
    rs = min(CONV_ROWS, rows)
    for r0 in range(0, rows, rs):
        for c0 in range(0, d, CONV_LANES):
            cs = slice(c0, c0 + CONV_LANES)
            c_ref[r0:r0 + rs, cs] = _dw_block(win_ref, off, r0, rs, cs, wdw_ref) + bdw_ref[:, cs]
    c = c_ref[...]
    xc = c - jnp.mean(c, axis=-1, keepdims=True)
    var = jnp.mean(xc * xc, axis=-1, keepdims=True)
    y = xc * lax.rsqrt(var + EPS) * lng_ref[...] + lnb_ref[...]
    return _silu(y).astype(BF16)


def _conv_prompt_kernel(halo_ref, u_ref, wdw_ref, bdw_ref, lng_ref, lnb_ref, o_ref, win_ref, c_ref, *, tm):
    first = pl.program_id(0) == 0
    halo = halo_ref[...]
    win_ref[0:HALO, :] = jnp.where(first, jnp.zeros_like(halo), halo)
    win_ref[HALO:HALO + tm, :] = u_ref[...]
    o_ref[...] = _dw_ln_silu(win_ref, HALO - CONV_PAD, tm, wdw_ref, bdw_ref, lng_ref, lnb_ref, c_ref)


def conv_prompt(u, wdw, bdw, lng, lnb, *, tm=256):
    d = u.shape[1]
    vec = pl.BlockSpec((1, d), lambda i: (0, 0))
    return pl.pallas_call(
        partial(_conv_prompt_kernel, tm=tm),
        grid=(SEQ // tm,),
        in_specs=[
            pl.BlockSpec((HALO, d), lambda i: (jnp.maximum(i * (tm // HALO) - 1, 0), 0)),
            pl.BlockSpec((tm, d), lambda i: (i, 0)),
            pl.BlockSpec((CONV_WIDTH, d), lambda i: (0, 0)),
            vec, vec, vec,
        ],
        out_specs=pl.BlockSpec((tm, d), lambda i: (i, 0)),
        out_shape=jax.ShapeDtypeStruct((SEQ, d), BF16),
        scratch_shapes=[pltpu.VMEM((HALO + tm, d), F32), pltpu.VMEM((tm, d), F32)],
        compiler_params=_params(("parallel",)),
        name="conv_prompt",
    )(u, u, wdw, bdw, lng, lnb)


def _conv_sample_kernel(up_ref, wdw_ref, bdw_ref, lng_ref, lnb_ref, o_ref, c_ref):
    o_ref[0] = _dw_ln_silu(up_ref.at[0], HALO - CONV_PAD, DEC_SEQ, wdw_ref, bdw_ref, lng_ref, lnb_ref, c_ref)


def conv_sample(up, wdw, bdw, lng, lnb):
    b, r, d = up.shape
    vec = pl.BlockSpec((1, d), lambda i: (0, 0))
    return pl.pallas_call(
        _conv_sample_kernel,
        grid=(b,),
        in_specs=[
            pl.BlockSpec((1, r, d), lambda i: (i, 0, 0)),
            pl.BlockSpec((CONV_WIDTH, d), lambda i: (0, 0)),
            vec, vec, vec,
        ],
        out_specs=pl.BlockSpec((1, DEC_SEQ, d), lambda i: (i, 0, 0)),
        out_shape=jax.ShapeDtypeStruct((b, DEC_SEQ, d), BF16),
        scratch_shapes=[pltpu.VMEM((DEC_SEQ, d), F32)],
        compiler_params=_params(("parallel",)),
        name="conv_sample",
    )(up, wdw, bdw, lng, lnb)


def _proj_res_kernel(a_ref, at_ref, w_ref, b_ref, x_ref, xt_ref, o_ref, wb_ref):
    j = pl.program_id(1)

    @pl.when(pl.program_id(0) == 0)
    def _():
        wb_ref[j] = w_ref[...].astype(BF16)

    w = wb_ref[j]
    b = b_ref[...]

    @pl.when(jnp.logical_not(_is_last_tile()))
    def _():
        o_ref[...] = x_ref[...] + (_dot(a_ref[...], w) + b)

    @pl.when(_is_last_tile())
    def _():
        o_ref[0:TAIL0] = x_ref[0:TAIL0] + (_dot(a_ref[0:TAIL0], w) + b)
        o_ref[TAIL0:TM] = xt_ref[...] + (_dot(at_ref[...], w) + b)


def proj_res(a_main, a_tail, w, layer, b, x_main, x_tail, tail_block, *, tn=512):
    k = a_main.shape[1]
    n = w.shape[2]
    nj = n // tn
    return pl.pallas_call(
        _proj_res_kernel,
        grid=(N_TILES, nj),
        in_specs=[
            _main_rows(k),
            pl.BlockSpec((M_SAMPLE, k), lambda i, j: (0, 0)),
            pl.BlockSpec((None, k, tn), lambda i, j: (layer, 0, jnp.where(i == 0, j, nj - 1))),
            pl.BlockSpec((1, tn), lambda i, j: (0, j)),
            pl.BlockSpec((TM, tn), lambda i, j: (i, j)),
            pl.BlockSpec((M_SAMPLE, tn), lambda i, j: (tail_block, j)),
        ],
        out_specs=pl.BlockSpec((TM, tn), lambda i, j: (i, j)),
        out_shape=jax.ShapeDtypeStruct((M_TOTAL, n), F32),
        scratch_shapes=[pltpu.VMEM((nj, k, tn), BF16)],
        compiler_params=_params(("arbitrary", "arbitrary")),
        name="proj_res",
    )(a_main, a_tail, w, b, x_main, x_tail)


def _ffn_step(x_ref, g_ref, wg_ref, wu_ref, wd_ref, o_ref, h_ref):
    @pl.when(pl.program_id(1) == 0)
    def _():
        x = x_ref[...]
        h_ref[...] = _rms_bf16(x, g_ref[...])
        o_ref[...] = x

    h = h_ref[...]
    gate = _dot(h, wg_ref[...].astype(BF16))
    up = _dot(h, wu_ref[...].astype(BF16))
    a = (_silu(gate) * up).astype(BF16)
    o_ref[...] += _dot(a, wd_ref[...].astype(BF16))


def _ffn_kernel(x_ref, g_ref, wg_ref, wu_ref, wd_ref, o_ref, h_ref):
    _ffn_step(x_ref, g_ref, wg_ref, wu_ref, wd_ref, o_ref, h_ref)


def _ffn_split_kernel(x_ref, g_ref, wg_ref, wu_ref, wd_ref, o_ref, ot_ref, h_ref):
    _ffn_step(x_ref, g_ref, wg_ref, wu_ref, wd_ref, o_ref, h_ref)

    @pl.when(_is_last_tile() & (pl.program_id(1) == pl.num_programs(1) - 1))
    def _():
        ot_ref[...] = o_ref[TAIL0:TM]


def ffn(x, g, wg, wu, wd, layer, *, split_out=False, tf=256):
    m, d = x.shape
    f = wg.shape[2]
    acc = pl.BlockSpec((TM, d), lambda i, j: (i, 0), pipeline_mode=pl.Buffered(1))
    in_specs = [
        _main_rows(d),
        pl.BlockSpec((1, d), lambda i, j: (0, 0)),
        pl.BlockSpec((None, d, tf), lambda i, j: (layer, 0, j)),
        pl.BlockSpec((None, d, tf), lambda i, j: (layer, 0, j)),
        pl.BlockSpec((None, tf, d), lambda i, j: (layer, j, 0)),
    ]
    common = dict(grid=(N_TILES, f // tf), in_specs=in_specs, scratch_shapes=[pltpu.VMEM((TM, d), BF16)])
    if split_out:
        return pl.pallas_call(
            _ffn_split_kernel,
            out_specs=[acc, pl.BlockSpec((M_SAMPLE, d), lambda i, j: (0, 0))],
            out_shape=[jax.ShapeDtypeStruct((SEQ, d), F32), jax.ShapeDtypeStruct((M_SAMPLE, d), F32)],
            compiler_params=_params(("arbitrary", "arbitrary")),
            name="ffn_split",
            **common,
        )(x, g, wg, wu, wd)
    return pl.pallas_call(
        _ffn_kernel,
        out_specs=acc,
        out_shape=jax.ShapeDtypeStruct((m, d), F32),
        compiler_params=_params(("parallel", "arbitrary")),
        name="ffn",
        **common,
    )(x, g, wg, wu, wd)


def _head_inv_rms(z, gsum_ref):
    sq = z * z
    hi = sq.astype(BF16)
    lo = (sq - hi.astype(F32)).astype(BF16)
    ss = _dot(hi, gsum_ref[...]) + _dot(lo, gsum_ref[...])
    return lax.rsqrt(ss * (1.0 / HEAD_DIM) + EPS)


def _expand_heads(r, gexp_ref):
    hi = r.astype(BF16)
    lo = (r - hi.astype(F32)).astype(BF16)
    return _dot(hi, gexp_ref[...]) + _dot(lo, gexp_ref[...])


def _rope_block(zb, cos, sin, first_half):
    partner = jnp.where(first_half, pltpu.roll(zb, LANES - HEAD_DIM // 2, 1), pltpu.roll(zb, HEAD_DIM // 2, 1))
    return zb * cos + partner * sin


def _rope(z, cos, sin):
    lane = lax.broadcasted_iota(jnp.int32, (z.shape[0], LANES), 1)
    first_half = (lane & (HEAD_DIM - 1)) < HEAD_DIM // 2
    blocks = [_rope_block(z[:, c:c + LANES], cos, sin, first_half) for c in range(0, z.shape[1], LANES)]
    return jnp.concatenate(blocks, axis=1)


KV_DUP = N_KV_HEADS * LANES
KV_PLAIN = N_KV_HEADS * HEAD_DIM


def _kv_kernel(x_ref, g_ref, w_ref, gsum_ref, gexp_ref, gk_ref, cos_ref, sin_ref,
               kf_ref, vf_ref, kb_ref, vt_ref):
    h = _rms_bf16(x_ref[...], g_ref[...])
    z = _dot(h, w_ref[...])
    zk = z[:, :KV_PLAIN]
    zv = z[:, KV_PLAIN:]
    r = _expand_heads(_head_inv_rms(zk, gsum_ref), gexp_ref)
    k = _rope(zk * r * gk_ref[...], cos_ref[...], sin_ref[...])
    kf_ref[...] = k
    vf_ref[...] = zv
    kb_ref[...] = k.astype(BF16)
    vt_ref[...] = zv.T.astype(BF16)


def kv_proj(x, g, w_kv, gsum, gexp, gk, cos, sin, *, tm=256):
    m, d = x.shape
    row = lambda w: pl.BlockSpec((tm, w), lambda i: (i, 0))
    full = lambda a: pl.BlockSpec(a.shape, lambda i: (0, 0))
    return pl.pallas_call(
        _kv_kernel,
        grid=(m // tm,),
        in_specs=[row(d), full(g), full(w_kv), full(gsum), full(gexp), full(gk), row(LANES), row(LANES)],
        out_specs=[row(KV_PLAIN), row(KV_PLAIN), row(KV_PLAIN), pl.BlockSpec((KV_PLAIN, tm), lambda i: (0, i))],
        out_shape=[
            jax.ShapeDtypeStruct((m, KV_PLAIN), F32),
            jax.ShapeDtypeStruct((m, KV_PLAIN), F32),
            jax.ShapeDtypeStruct((m, KV_PLAIN), BF16),
            jax.ShapeDtypeStruct((KV_PLAIN, m), BF16),
        ],
        compiler_params=_params(("parallel",)),
        name="kv_proj",
    )(x, g, w_kv, gsum, gexp, gk, cos, sin)


Q_GROUP_ROWS = 8 * HEAD_DIM


def _q_kernel(x_ref, g_ref, w_ref, gq_ref, cos_ref, sin_ref, qt_ref, wb_ref):
    @pl.when(pl.program_id(0) == 0)
    def _():
        wb_ref[...] = w_ref[...].astype(BF16)

    tm = x_ref.shape[0]
    half = HEAD_DIM // 2
    h = _rms_bf16(x_ref[...], g_ref[...])
    gq = gq_ref[...][None]
    cos = cos_ref[...][None]
    sin = sin_ref[...][None]
    for r0 in range(0, N_HEADS * HEAD_DIM, Q_GROUP_ROWS):
        rows = slice(r0, r0 + Q_GROUP_ROWS)
        zt = lax.dot_general(wb_ref[rows, :], h, (((1,), (1,)), ((), ())), preferred_element_type=F32)
        z3 = zt.reshape(Q_GROUP_ROWS // HEAD_DIM, HEAD_DIM, tm)
        r = lax.rsqrt(jnp.mean(z3 * z3, axis=1, keepdims=True) + EPS)
        zn = z3 * r * gq
        x1 = zn[:, :half]
        x2 = zn[:, half:]
        q3 = jnp.concatenate([x1 * cos - x2 * sin, x2 * cos + x1 * sin], axis=1) * SCALE
        qt_ref[rows, :] = q3.reshape(Q_GROUP_ROWS, tm).astype(BF16)


def q_proj(x, g, wt, layer, gq, cos_t, sin_t, *, tm=256):
    m, d = x.shape
    n = wt.shape[1]
    full = lambda a: pl.BlockSpec(a.shape, lambda i: (0, 0))
    cols = lambda rows: pl.BlockSpec((rows, tm), lambda i: (0, i))
    return pl.pallas_call(
        _q_kernel,
        grid=(m // tm,),
        in_specs=[
            pl.BlockSpec((tm, d), lambda i: (i, 0)), full(g),
            pl.BlockSpec((None, n, d), lambda i: (layer, 0, 0), pipeline_mode=pl.Buffered(1)),
            full(gq), cols(HEAD_DIM // 2), cols(HEAD_DIM // 2),
        ],
        out_specs=cols(n),
        out_shape=jax.ShapeDtypeStruct((n, m), BF16),
        scratch_shapes=[pltpu.VMEM((n, d), BF16)],
        compiler_params=_params(("arbitrary",)),
        name="q_proj",
    )(x, g, wt, gq, cos_t, sin_t)


TQ = 2 * CHUNK
TK = 2 * TQ


def _attn_prompt_kernel(sink_ref, qt_ref, kp_ref, kc_ref, vp_ref, vc_ref, o_ref, ot_ref):
    i = pl.program_id(0)
    cols = GROUP * TQ
    key = lax.broadcasted_iota(jnp.int32, (TK, cols), 0)
    qry = lax.broadcasted_iota(jnp.int32, (TK, cols), 1)
    k_chunk = key >> 6
    q_chunk = (qry & (TQ - 1)) >> 6
    rel = k_chunk - 2 - q_chunk
    visible = (rel <= 0) & (rel >= -2) & ((i > 0) | (k_chunk >= 2))
    bias = jnp.where(visible, 0.0, -jnp.inf).astype(F32)
    zeros = jnp.zeros((HEAD_DIM, cols), BF16)
    for kh in range(N_KV_HEADS):
        pair = slice((kh // 2) * LANES, (kh // 2 + 1) * LANES)
        kk = jnp.concatenate([kp_ref[:, pair], kc_ref[:, pair]], axis=0)
        heads = range(kh * GROUP, (kh + 1) * GROUP)
        qg = jnp.concatenate([qt_ref[h * HEAD_DIM:(h + 1) * HEAD_DIM, :] for h in heads], axis=1)
        rhs = jnp.concatenate([qg, zeros] if kh % 2 == 0 else [zeros, qg], axis=0)
        s = _dot(kk, rhs) + bias
        sink = jnp.concatenate([jnp.full((1, TQ), sink_ref[h], F32) for h in heads], axis=1)
        m = jnp.maximum(jnp.max(s, axis=0, keepdims=True), sink)
        p = jnp.exp(s - m)
        denom = jnp.sum(p, axis=0, keepdims=True) + jnp.exp(sink - m)
        rows = slice(kh * HEAD_DIM, (kh + 1) * HEAD_DIM)
        vt = jnp.concatenate([vp_ref[rows, :], vc_ref[rows, :]], axis=1)
        o = _dot(vt, p.astype(BF16)) / denom
        for g, h in enumerate(heads):
            ot_ref[h * HEAD_DIM:(h + 1) * HEAD_DIM, :] = o[:, g * TQ:(g + 1) * TQ]
    o_ref[...] = ot_ref[...].T.astype(BF16)


def attn_prompt(sinks, qt, kb, vt):
    hq = qt.shape[0]
    kcur = pl.BlockSpec((TQ, KV_PLAIN), lambda i: (i, 0))
    kprev = pl.BlockSpec((TQ, KV_PLAIN), lambda i: (jnp.maximum(i - 1, 0), 0))
    vcur = pl.BlockSpec((KV_PLAIN, TQ), lambda i: (0, i))
    vprev = pl.BlockSpec((KV_PLAIN, TQ), lambda i: (0, jnp.maximum(i - 1, 0)))
    return pl.pallas_call(
        _attn_prompt_kernel,
        grid=(SEQ // TQ,),
        in_specs=[
            pl.BlockSpec(memory_space=pltpu.SMEM),
            pl.BlockSpec((hq, TQ), lambda i: (0, i)), kprev, kcur, vprev, vcur,
        ],
        out_specs=pl.BlockSpec((TQ, hq), lambda i: (i, 0)),
        out_shape=jax.ShapeDtypeStruct((SEQ, hq), BF16),
        scratch_shapes=[pltpu.VMEM((hq, TQ), F32)],
        compiler_params=_params(("parallel",)),
        name="attn_prompt",
    )(sinks, qt, kb, kb, vt, vt)


def _attend_group(q_ref, sink_ref, kh, kk, vv, bias, rows):
    lane = lax.broadcasted_iota(jnp.int32, (rows, LANES), 1)
    low = lane < HEAD_DIM
    qs, sc = [], []
    for pair in range(GROUP // 2):
        c0 = (kh * (GROUP // 2) + pair) * LANES
        qp = q_ref[:, c0:c0 + LANES]
        qs.append(jnp.where(low, qp, jnp.zeros_like(qp)))
        qs.append(jnp.where(low, jnp.zeros_like(qp), qp))
        for half in range(2):
            head = kh * GROUP + 2 * pair + half
            sc.append(jnp.full((rows, 1), sink_ref[head], F32))
    qs = jnp.concatenate(qs, axis=0)
    sink = jnp.concatenate(sc, axis=0)
    s = lax.dot_general(qs, kk, (((1,), (1,)), ((), ())), preferred_element_type=F32)
    s = s + bias
    m = jnp.maximum(jnp.max(s, axis=-1, keepdims=True), sink)
    p = jnp.exp(s - m)
    denom = jnp.sum(p, axis=-1, keepdims=True) + jnp.exp(sink - m)
    o = _dot(p.astype(BF16), vv) / denom
    outs = []
    for pair in range(GROUP // 2):
        o0 = o[(2 * pair) * rows:(2 * pair + 1) * rows]
        o1 = o[(2 * pair + 1) * rows:(2 * pair + 2) * rows]
        outs.append(jnp.where(low, o0, o1))
    return jnp.concatenate(outs, axis=1).astype(BF16)


KEYS_SAMPLE = 256


def _attn_sample_kernel(sink_ref, q_ref, k_ref, v_ref, o_ref):
    col = lax.broadcasted_iota(jnp.int32, (GROUP * DEC_SEQ, KEYS_SAMPLE), 1)
    bias = jnp.where(col < WINDOW + DEC_SEQ, 0.0, -jnp.inf).astype(F32)
    for kh in range(N_KV_HEADS):
        sl = slice(kh * LANES, (kh + 1) * LANES)
        o_ref[:, kh * GROUP * HEAD_DIM:(kh + 1) * GROUP * HEAD_DIM] = _attend_group(
            q_ref, sink_ref, kh, k_ref[0, :, sl], v_ref[0, :, sl], bias, DEC_SEQ)


def attn_sample(sinks, q, k2, v2):
    hq = q.shape[1]
    row = pl.BlockSpec((DEC_SEQ, hq), lambda b: (b, 0))
    keys = pl.BlockSpec((1, KEYS_SAMPLE, KV_DUP), lambda b: (b, 0, 0))
    return pl.pallas_call(
        _attn_sample_kernel,
        grid=(DEC_BATCH,),
        in_specs=[pl.BlockSpec(memory_space=pltpu.SMEM), row, keys, keys],
        out_specs=row,
        out_shape=jax.ShapeDtypeStruct((M_SAMPLE, hq), BF16),
        compiler_params=_params(("parallel",)),
        name="attn_sample",
    )(sinks, q, k2, v2)


def _rope_tables(pos):
    half = HEAD_DIM // 2
    inv_freq = ROPE_THETA ** (-jnp.arange(half, dtype=F32) / half)
    ang = pos.astype(F32)[:, None] * inv_freq[None, :]
    cos = jnp.cos(ang)
    sin = jnp.sin(ang)
    cos_l = jnp.tile(jnp.concatenate([cos, cos], axis=1), (1, LANES // HEAD_DIM))
    sin_l = jnp.tile(jnp.concatenate([-sin, sin], axis=1), (1, LANES // HEAD_DIM))
    return cos_l, sin_l, cos.T, sin.T


def _head_indicator(n):
    head = jnp.arange(n, dtype=jnp.int32)[:, None] // HEAD_DIM
    return (head == jnp.arange(LANES, dtype=jnp.int32)[None, :]).astype(BF16)


def _dup_heads(w):
    lead = w.shape[:-1]
    w4 = w.reshape(lead + (N_KV_HEADS, 1, HEAD_DIM))
    return jnp.broadcast_to(w4, lead + (N_KV_HEADS, 2, HEAD_DIM)).reshape(lead + (KV_DUP,))


def kernel(x_prompt, x_sample, state_conv, cache_k, cache_v, ffn_norm, w_gate, w_up, w_down,
           conv_norm, w_pw1, b_pw1, w_dw, b_dw, conv_ln_g, conv_ln_b, w_pw2, b_pw2,
           kv_norm, w_k, w_v, k_norm, attn_norm, w_q, q_norm, sinks, w_o):
    d = D_MODEL
    hq = N_HEADS * HEAD_DIM
    x = (x_prompt.reshape(SEQ, d), x_sample.reshape(M_SAMPLE, d), 0)

    pos = jnp.concatenate([jnp.arange(SEQ), jnp.tile(PAST_LEN + jnp.arange(DEC_SEQ), DEC_BATCH)])
    cos, sin, cos_t, sin_t = _rope_tables(pos)
    gsum = _head_indicator(KV_PLAIN)
    gexp = gsum.T
    zero_bias = jnp.zeros((1, d), F32)
    wq_t = jnp.swapaxes(w_q, 1, 2)
    q_tile = 256

    conv_p, conv_s = [], []
    for i in range(DEPTH):
        if i < N_A:
            u = pw1_glu(*x, conv_norm[i][None], w_pw1, i, b_pw1[i][None])
            u_s = u[SEQ:].reshape(DEC_BATCH, DEC_SEQ, d)
            up_s = jnp.concatenate(
                [jnp.zeros((DEC_BATCH, HALO - CONV_PAD, d), F32), state_conv[i], u_s], axis=1)
            cargs = (w_dw[i], b_dw[i][None], conv_ln_g[i][None], conv_ln_b[i][None])
            cn_p = conv_prompt(u, *cargs)
            cn_s = conv_sample(up_s, *cargs).reshape(M_SAMPLE, d)
            y = proj_res(cn_p, cn_s, w_pw2, i, b_pw2[i][None], *x)
            conv_p.append(u[SEQ - CONV_PAD:SEQ][None])
            conv_s.append(up_s[:, -CONV_PAD:])
        else:
            j = i - N_A
            xs = x[0]
            gq = jnp.broadcast_to(q_norm[j][:, None], (HEAD_DIM, q_tile))
            qt = q_proj(xs, attn_norm[j][None], wq_t, j, gq, cos_t, sin_t, tm=q_tile)
            o_p = attn_prompt(sinks[j], qt, kb, vt)
            o_s = attn_sample(sinks[j], qt[:, SEQ:].T, k2_s, v2_s)
            y = proj_res(o_p, o_s, w_o, j, zero_bias, *x)
        if i < DEPTH - 1:
            xs = ffn(y, ffn_norm[i][None], w_gate, w_up, w_down, i)
            x = (xs, xs, TAIL_BLOCK)
        else:
            y_p, y_s = ffn(y, ffn_norm[i][None], w_gate, w_up, w_down, i, split_out=True)
        if i == N_A - 1:
            w_kv = jnp.concatenate([w_k, w_v], axis=1).astype(BF16)
            gk = jnp.tile(k_norm, N_KV_HEADS)[None]
            kf, vf, kb, vt = kv_proj(xs, kv_norm[None], w_kv, gsum, gexp, gk, cos, sin)
            pad = jnp.zeros((DEC_BATCH, KEYS_SAMPLE - WINDOW - DEC_SEQ, KV_PLAIN), F32)
            dup_keys = lambda cache, new: _dup_heads(jnp.concatenate(
                [cache.reshape(DEC_BATCH, WINDOW, KV_PLAIN), new[SEQ:].reshape(DEC_BATCH, DEC_SEQ, KV_PLAIN), pad],
                axis=1)).astype(BF16)
            k2_s = dup_keys(cache_k, kf)
            v2_s = dup_keys(cache_v, vf)

    kv4 = lambda a, b: a.reshape(b, -1, N_KV_HEADS, HEAD_DIM)
    k_p = kv4(kf[SEQ - WINDOW:SEQ], 1)
    v_p = kv4(vf[SEQ - WINDOW:SEQ], 1)
    k_s = jnp.concatenate([cache_k, kv4(kf[SEQ:], DEC_BATCH)], axis=1)[:, -WINDOW:]
    v_s = jnp.concatenate([cache_v, kv4(vf[SEQ:], DEC_BATCH)], axis=1)[:, -WINDOW:]
    return (y_p.reshape(1, SEQ, d), y_s.reshape(DEC_BATCH, DEC_SEQ, d),
            jnp.stack(conv_p), k_p, v_p, jnp.stack(conv_s), k_s, v_s)
```

```python
from functools import partial

import jax
import jax.numpy as jnp
from jax import lax
from jax.experimental import pallas as pl
from jax.experimental.pallas import tpu as pltpu

D_MODEL = 2048
SEQ = 8192
DEPTH = 4
DEC_BATCH = 16
DEC_SEQ = 16
PAST_LEN = 2048
CHUNK = 64
N_A = DEPTH // 2
CONV_WIDTH = 31
CONV_PAD = CONV_WIDTH - 1
N_HEADS = 32
N_KV_HEADS = 4
HEAD_DIM = 64
GROUP = N_HEADS // N_KV_HEADS
WINDOW = 128
D_FF = 5632
ROPE_THETA = 10000.0
EPS = 1e-6
SCALE = HEAD_DIM ** -0.5

M_SAMPLE = DEC_BATCH * DEC_SEQ
M_TOTAL = SEQ + M_SAMPLE

LANES = 128
SUBLANES = 8
HALO = 32
VMEM_LIMIT = 56 * 1024 * 1024

TM = 1056
N_TILES = M_TOTAL // TM
TAIL0 = SEQ - (N_TILES - 1) * TM
TAIL_BLOCK = SEQ // M_SAMPLE

BF16 = jnp.bfloat16
F32 = jnp.float32


def _params(semantics):
    return pltpu.CompilerParams(dimension_semantics=semantics, vmem_limit_bytes=VMEM_LIMIT)


def _rms_bf16(x, g):
    ms = jnp.mean(x * x, axis=-1, keepdims=True)
    return ((x * lax.rsqrt(ms + EPS)) * g).astype(BF16)


def _dot(a, b):
    return jnp.dot(a, b, preferred_element_type=F32)


def _silu(x):
    return x * jax.nn.sigmoid(x)


def _is_last_tile():
    return pl.program_id(0) == pl.num_programs(0) - 1


def _main_rows(width):
    return pl.BlockSpec((TM, width), lambda i, j: (i, 0))


def _pw1_glu_kernel(x_ref, xt_ref, g_ref, wa_ref, wb_ref, ba_ref, bb_ref, u_ref, h_ref):
    @pl.when(pl.program_id(1) == 0)
    def _():
        g = g_ref[...]
        h_ref[0:TAIL0] = _rms_bf16(x_ref[0:TAIL0], g)

        @pl.when(jnp.logical_not(_is_last_tile()))
        def _():
            h_ref[TAIL0:TM] = _rms_bf16(x_ref[TAIL0:TM], g)

        @pl.when(_is_last_tile())
        def _():
            h_ref[TAIL0:TM] = _rms_bf16(xt_ref[...], g)

    h = h_ref[...]
    a = _dot(h, wa_ref[...].astype(BF16)) + ba_ref[...]
    b = _dot(h, wb_ref[...].astype(BF16)) + bb_ref[...]
    u_ref[...] = a * jax.nn.sigmoid(b)


def pw1_glu(x_main, x_tail, tail_block, g, w, layer, b, *, tn=256):
    d = x_main.shape[1]
    nj = d // tn
    return pl.pallas_call(
        _pw1_glu_kernel,
        grid=(N_TILES, nj),
        in_specs=[
            _main_rows(d),
            pl.BlockSpec((M_SAMPLE, d), lambda i, j: (tail_block, 0)),
            pl.BlockSpec((1, d), lambda i, j: (0, 0)),
            pl.BlockSpec((None, d, tn), lambda i, j: (layer, 0, j)),
            pl.BlockSpec((None, d, tn), lambda i, j: (layer, 0, j + nj)),
            pl.BlockSpec((1, tn), lambda i, j: (0, j)),
            pl.BlockSpec((1, tn), lambda i, j: (0, j + nj)),
        ],
        out_specs=pl.BlockSpec((TM, tn), lambda i, j: (i, j)),
        out_shape=jax.ShapeDtypeStruct((M_TOTAL, d), F32),
        scratch_shapes=[pltpu.VMEM((TM, d), BF16)],
        compiler_params=_params(("parallel", "arbitrary")),
        name="pw1_glu",
    )(x_main, x_tail, g, w, w, b, b)


CONV_ROWS = 64
CONV_LANES = 256


def _dw_block(win_ref, off, r0, rs, cs, wdw_ref):
    acc = None
    for r in range(SUBLANES):
        es = [e for e in range(off, off + CONV_WIDTH) if e % SUBLANES == r]
        rows = rs + SUBLANES if r else rs
        a = None
        for e in es:
            j = e - off
            term = win_ref[pl.ds(r0 + e - r, rows), cs] * wdw_ref[j:j + 1, cs]
            a = term if a is None else a + term
        if r:
            a = pltpu.roll(a, rows - r, 0)[0:rs]
        acc = a if acc is None else acc + a
    return acc


def _dw_ln_silu(win_ref, off, rows, wdw_ref, bdw_ref, lng_ref, lnb_ref, c_ref):
    d = c_ref.shape[1]
    rs = min(CONV_ROWS, rows)
    for r0 in range(0, rows, rs):
        for c0 in range(0, d, CONV_LANES):
            cs = slice(c0, c0 + CONV_LANES)
            c_ref[r0:r0 + rs, cs] = _dw_block(win_ref, off, r0, rs, cs, wdw_ref) + bdw_ref[:, cs]
    c = c_ref[...]
    xc = c - jnp.mean(c, axis=-1, keepdims=True)
    var = jnp.mean(xc * xc, axis=-1, keepdims=True)
    y = xc * lax.rsqrt(var + EPS) * lng_ref[...] + lnb_ref[...]
    return _silu(y).astype(BF16)


def _conv_prompt_kernel(halo_ref, u_ref, wdw_ref, bdw_ref, lng_ref, lnb_ref, o_ref, win_ref, c_ref, *, tm):
    first = pl.program_id(0) == 0
    halo = halo_ref[...]
    win_ref[0:HALO, :] = jnp.where(first, jnp.zeros_like(halo), halo)
    win_ref[HALO:HALO + tm, :] = u_ref[...]
    o_ref[...] = _dw_ln_silu(win_ref, HALO - CONV_PAD, tm, wdw_ref, bdw_ref, lng_ref, lnb_ref, c_ref)


def conv_prompt(u, wdw, bdw, lng, lnb, *, tm=256):
    d = u.shape[1]
    vec = pl.BlockSpec((1, d), lambda i: (0, 0))
    return pl.pallas_call(
        partial(_conv_prompt_kernel, tm=tm),
        grid=(SEQ // tm,),
        in_specs=[
            pl.BlockSpec((HALO, d), lambda i: (jnp.maximum(i * (tm // HALO) - 1, 0), 0)),
            pl.BlockSpec((tm, d), lambda i: (i, 0)),
            pl.BlockSpec((CONV_WIDTH, d), lambda i: (0, 0)),
            vec, vec, vec,
        ],
        out_specs=pl.BlockSpec((tm, d), lambda i: (i, 0)),
        out_shape=jax.ShapeDtypeStruct((SEQ, d), BF16),
        scratch_shapes=[pltpu.VMEM((HALO + tm, d), F32), pltpu.VMEM((tm, d), F32)],
        compiler_params=_params(("parallel",)),
        name="conv_prompt",
    )(u, u, wdw, bdw, lng, lnb)


def _conv_sample_kernel(up_ref, wdw_ref, bdw_ref, lng_ref, lnb_ref, o_ref, c_ref):
    o_ref[0] = _dw_ln_silu(up_ref.at[0], HALO - CONV_PAD, DEC_SEQ, wdw_ref, bdw_ref, lng_ref, lnb_ref, c_ref)


def conv_sample(up, wdw, bdw, lng, lnb):
    b, r, d = up.shape
    vec = pl.BlockSpec((1, d), lambda i: (0, 0))
    return pl.pallas_call(
        _conv_sample_kernel,
        grid=(b,),
        in_specs=[
            pl.BlockSpec((1, r, d), lambda i: (i, 0, 0)),
            pl.BlockSpec((CONV_WIDTH, d), lambda i: (0, 0)),
            vec, vec, vec,
        ],
        out_specs=pl.BlockSpec((1, DEC_SEQ, d), lambda i: (i, 0, 0)),
        out_shape=jax.ShapeDtypeStruct((b, DEC_SEQ, d), BF16),
        scratch_shapes=[pltpu.VMEM((DEC_SEQ, d), F32)],
        compiler_params=_params(("parallel",)),
        name="conv_sample",
    )(up, wdw, bdw, lng, lnb)


PR_TM = 528
PR_TILES = M_TOTAL // PR_TM
PR_TAIL0 = SEQ - (PR_TILES - 1) * PR_TM
PR_CAST_COLS = 512


def _proj_res_kernel(a_ref, at_ref, w_ref, b_ref, x_ref, xt_ref, o_ref, wb_ref):
    @pl.when(pl.program_id(0) == 0)
    def _():
        for c in range(0, wb_ref.shape[1], PR_CAST_COLS):
            wb_ref[:, c:c + PR_CAST_COLS] = w_ref[:, c:c + PR_CAST_COLS].astype(BF16)

    b = b_ref[...]

    @pl.when(jnp.logical_not(_is_last_tile()))
    def _():
        o_ref[...] = x_ref[...] + (_dot(a_ref[...], wb_ref[...]) + b)

    @pl.when(_is_last_tile())
    def _():
        o_ref[0:PR_TAIL0] = x_ref[0:PR_TAIL0] + (_dot(a_ref[0:PR_TAIL0], wb_ref[...]) + b)
        o_ref[PR_TAIL0:PR_TM] = xt_ref[...] + (_dot(at_ref[...], wb_ref[...]) + b)


def proj_res(a_main, a_tail, w, layer, b, x_main, x_tail, tail_block):
    k = a_main.shape[1]
    n = w.shape[2]
    rows = lambda width: pl.BlockSpec((PR_TM, width), lambda i: (i, 0))
    return pl.pallas_call(
        _proj_res_kernel,
        grid=(PR_TILES,),
        in_specs=[
            rows(k),
            pl.BlockSpec((M_SAMPLE, k), lambda i: (0, 0), pipeline_mode=pl.Buffered(1)),
            pl.BlockSpec((None, k, n), lambda i: (layer, 0, 0), pipeline_mode=pl.Buffered(1)),
            pl.BlockSpec((1, n), lambda i: (0, 0)),
            rows(n),
            pl.BlockSpec((M_SAMPLE, n), lambda i: (tail_block, 0), pipeline_mode=pl.Buffered(1)),
        ],
        out_specs=rows(n),
        out_shape=jax.ShapeDtypeStruct((M_TOTAL, n), F32),
        scratch_shapes=[pltpu.VMEM((k, n), BF16)],
        compiler_params=_params(("arbitrary",)),
        name="proj_res",
    )(a_main, a_tail, w, b, x_main, x_tail)


FFN_TM = 1408
FFN_TILES = M_TOTAL // FFN_TM
FFN_TAIL0 = SEQ - (FFN_TILES - 1) * FFN_TM


def _ffn_kernel(x_hbm, g_ref, wg_ref, wu_ref, wd_ref, *rest, split):
    if split:
        o_hbm, ot_hbm, acc_ref, h_ref, in_sem, out_sem = rest
    else:
        o_hbm, acc_ref, h_ref, in_sem, out_sem = rest
        ot_hbm = None
    i = pl.program_id(0)
    j = pl.program_id(1)
    nt = pl.num_programs(0)
    nj = pl.num_programs(1)
    slot = lax.rem(i, 2)
    other = 1 - slot
    last = i == nt - 1

    def x_copy(tile, s):
        return pltpu.make_async_copy(x_hbm.at[pl.ds(tile * FFN_TM, FFN_TM)], acc_ref.at[s], in_sem.at[s])

    def out_copy(tile, s):
        return pltpu.make_async_copy(acc_ref.at[s], o_hbm.at[pl.ds(tile * FFN_TM, FFN_TM)], out_sem.at[s])

    def last_copies(s):
        if not split:
            return [out_copy(nt - 1, s)]
        return [
            pltpu.make_async_copy(acc_ref.at[s, pl.ds(0, FFN_TAIL0)],
                                  o_hbm.at[pl.ds((FFN_TILES - 1) * FFN_TM, FFN_TAIL0)], out_sem.at[s]),
            pltpu.make_async_copy(acc_ref.at[s, pl.ds(FFN_TAIL0, M_SAMPLE)], ot_hbm, out_sem.at[s]),
        ]

    @pl.when((i == 0) & (j == 0))
    def _():
        x_copy(0, 0).start()

    @pl.when(j == 0)
    def _():
        x_copy(i, slot).wait()
        h_ref[...] = _rms_bf16(acc_ref[slot], g_ref[...])

    @pl.when((j == nj // 2) & jnp.logical_not(last))
    def _():
        @pl.when(i >= 1)
        def _():
            out_copy(i - 1, other).wait()

        x_copy(i + 1, other).start()

    h = h_ref[...]
    gate = _dot(h, wg_ref[...].astype(BF16))
    up = _dot(h, wu_ref[...].astype(BF16))
    a = (_silu(gate) * up).astype(BF16)
    acc_ref[slot] += _dot(a, wd_ref[...].astype(BF16))

    @pl.when(j == nj - 1)
    def _():
        @pl.when(jnp.logical_not(last))
        def _():
            out_copy(i, slot).start()

        @pl.when(last)
        def _():
            out_copy(i - 1, other).wait()
            for c in last_copies(slot):
                c.start()
            for c in last_copies(slot):
                c.wait()


def ffn(x, g, wg, wu, wd, layer, *, split_out=False, tf=256):
    m, d = x.shape
    f = wg.shape[2]
    hbm = pl.BlockSpec(memory_space=pl.ANY)
    if split_out:
        out_specs = [hbm, hbm]
        out_shape = [jax.ShapeDtypeStruct((SEQ, d), F32), jax.ShapeDtypeStruct((M_SAMPLE, d), F32)]
    else:
        out_specs = hbm
        out_shape = jax.ShapeDtypeStruct((m, d), F32)
    return pl.pallas_call(
        partial(_ffn_kernel, split=split_out),
        grid=(FFN_TILES, f // tf),
        in_specs=[
            hbm,
            pl.BlockSpec((1, d), lambda i, j: (0, 0)),
            pl.BlockSpec((None, d, tf), lambda i, j: (layer, 0, j)),
            pl.BlockSpec((None, d, tf), lambda i, j: (layer, 0, j)),
            pl.BlockSpec((None, tf, d), lambda i, j: (layer, j, 0)),
        ],
        out_specs=out_specs,
        out_shape=out_shape,
        scratch_shapes=[
            pltpu.VMEM((2, FFN_TM, d), F32),
            pltpu.VMEM((FFN_TM, d), BF16),
            pltpu.SemaphoreType.DMA((2,)),
            pltpu.SemaphoreType.DMA((2,)),
        ],
        compiler_params=_params(("arbitrary", "arbitrary")),
        name="ffn_split" if split_out else "ffn",
    )(x, g, wg, wu, wd)


def _head_inv_rms(z, gsum_ref):
    sq = z * z
    hi = sq.astype(BF16)
    lo = (sq - hi.astype(F32)).astype(BF16)
    ss = _dot(hi, gsum_ref[...]) + _dot(lo, gsum_ref[...])
    return lax.rsqrt(ss * (1.0 / HEAD_DIM) + EPS)


def _expand_heads(r, gexp_ref):
    hi = r.astype(BF16)
    lo = (r - hi.astype(F32)).astype(BF16)
    return _dot(hi, gexp_ref[...]) + _dot(lo, gexp_ref[...])


def _rope_block(zb, cos, sin, first_half):
    partner = jnp.where(first_half, pltpu.roll(zb, LANES - HEAD_DIM // 2, 1), pltpu.roll(zb, HEAD_DIM // 2, 1))
    return zb * cos + partner * sin


def _rope(z, cos, sin):
    lane = lax.broadcasted_iota(jnp.int32, (z.shape[0], LANES), 1)
    first_half = (lane & (HEAD_DIM - 1)) < HEAD_DIM // 2
    blocks = [_rope_block(z[:, c:c + LANES], cos, sin, first_half) for c in range(0, z.shape[1], LANES)]
    return jnp.concatenate(blocks, axis=1)


KV_DUP = N_KV_HEADS * LANES
KV_PLAIN = N_KV_HEADS * HEAD_DIM


def _kv_kernel(x_ref, g_ref, w_ref, gsum_ref, gexp_ref, gk_ref, cos_ref, sin_ref,
               kf_ref, vf_ref, kb_ref, vt_ref):
    h = _rms_bf16(x_ref[...], g_ref[...])
    z = _dot(h, w_ref[...])
    zk = z[:, :KV_PLAIN]
    zv = z[:, KV_PLAIN:]
    r = _expand_heads(_head_inv_rms(zk, gsum_ref), gexp_ref)
    k = _rope(zk * r * gk_ref[...], cos_ref[...], sin_ref[...])
    kf_ref[...] = k
    vf_ref[...] = zv
    kb_ref[...] = k.astype(BF16)
    vt_ref[...] = zv.T.astype(BF16)


def kv_proj(x, g, w_kv, gsum, gexp, gk, cos, sin, *, tm=256):
    m, d = x.shape
    row = lambda w: pl.BlockSpec((tm, w), lambda i: (i, 0))
    full = lambda a: pl.BlockSpec(a.shape, lambda i: (0, 0))
    return pl.pallas_call(
        _kv_kernel,
        grid=(m // tm,),
        in_specs=[row(d), full(g), full(w_kv), full(gsum), full(gexp), full(gk), row(LANES), row(LANES)],
        out_specs=[row(KV_PLAIN), row(KV_PLAIN), row(KV_PLAIN), pl.BlockSpec((KV_PLAIN, tm), lambda i: (0, i))],
        out_shape=[
            jax.ShapeDtypeStruct((m, KV_PLAIN), F32),
            jax.ShapeDtypeStruct((m, KV_PLAIN), F32),
            jax.ShapeDtypeStruct((m, KV_PLAIN), BF16),
            jax.ShapeDtypeStruct((KV_PLAIN, m), BF16),
        ],
        compiler_params=_params(("parallel",)),
        name="kv_proj",
    )(x, g, w_kv, gsum, gexp, gk, cos, sin)


Q_GROUP_ROWS = 8 * HEAD_DIM


def _q_kernel(x_ref, g_ref, w_ref, gq_ref, cos_ref, sin_ref, qt_ref, wb_ref):
    @pl.when(pl.program_id(0) == 0)
    def _():
        wb_ref[...] = w_ref[...].T.astype(BF16)

    tm = x_ref.shape[0]
    half = HEAD_DIM // 2
    h = _rms_bf16(x_ref[...], g_ref[...])
    gq = gq_ref[...][None]
    cos = cos_ref[...][None]
    sin = sin_ref[...][None]
    for r0 in range(0, N_HEADS * HEAD_DIM, Q_GROUP_ROWS):
        rows = slice(r0, r0 + Q_GROUP_ROWS)
        zt = lax.dot_general(wb_ref[rows, :], h, (((1,), (1,)), ((), ())), preferred_element_type=F32)
        z3 = zt.reshape(Q_GROUP_ROWS // HEAD_DIM, HEAD_DIM, tm)
        r = lax.rsqrt(jnp.mean(z3 * z3, axis=1, keepdims=True) + EPS)
        zn = z3 * r * gq
        x1 = zn[:, :half]
        x2 = zn[:, half:]
        q3 = jnp.concatenate([x1 * cos - x2 * sin, x2 * cos + x1 * sin], axis=1) * SCALE
        qt_ref[rows, :] = q3.reshape(Q_GROUP_ROWS, tm).astype(BF16)


def q_proj(x, g, w, layer, gq, cos_t, sin_t, *, tm=256):
    m, d = x.shape
    n = w.shape[2]
    full = lambda a: pl.BlockSpec(a.shape, lambda i: (0, 0))
    cols = lambda rows: pl.BlockSpec((rows, tm), lambda i: (0, i))
    return pl.pallas_call(
        _q_kernel,
        grid=(m // tm,),
        in_specs=[
            pl.BlockSpec((tm, d), lambda i: (i, 0)), full(g),
            pl.BlockSpec((None, d, n), lambda i: (layer, 0, 0), pipeline_mode=pl.Buffered(1)),
            full(gq), cols(HEAD_DIM // 2), cols(HEAD_DIM // 2),
        ],
        out_specs=cols(n),
        out_shape=jax.ShapeDtypeStruct((n, m), BF16),
        scratch_shapes=[pltpu.VMEM((n, d), BF16)],
        compiler_params=_params(("arbitrary",)),
        name="q_proj",
    )(x, g, w, gq, cos_t, sin_t)


TQ = 2 * CHUNK
TK = 2 * TQ


def _attn_prompt_kernel(sink_ref, qt_ref, kp_ref, kc_ref, vp_ref, vc_ref, o_ref, ot_ref):
    i = pl.program_id(0)
    cols = GROUP * TQ
    key = lax.broadcasted_iota(jnp.int32, (TK, cols), 0)
    qry = lax.broadcasted_iota(jnp.int32, (TK, cols), 1)
    k_chunk = key >> 6
    q_chunk = (qry & (TQ - 1)) >> 6
    rel = k_chunk - 2 - q_chunk
    visible = (rel <= 0) & (rel >= -2) & ((i > 0) | (k_chunk >= 2))
    bias = jnp.where(visible, 0.0, -jnp.inf).astype(F32)
    zeros = jnp.zeros((HEAD_DIM, cols), BF16)
    for kh in range(N_KV_HEADS):
        pair = slice((kh // 2) * LANES, (kh // 2 + 1) * LANES)
        kk = jnp.concatenate([kp_ref[:, pair], kc_ref[:, pair]], axis=0)
        heads = range(kh * GROUP, (kh + 1) * GROUP)
        qg = jnp.concatenate([qt_ref[h * HEAD_DIM:(h + 1) * HEAD_DIM, :] for h in heads], axis=1)
        rhs = jnp.concatenate([qg, zeros] if kh % 2 == 0 else [zeros, qg], axis=0)
        s = _dot(kk, rhs) + bias
        sink = jnp.concatenate([jnp.full((1, TQ), sink_ref[h], F32) for h in heads], axis=1)
        m = jnp.maximum(jnp.max(s, axis=0, keepdims=True), sink)
        p = jnp.exp(s - m)
        denom = jnp.sum(p, axis=0, keepdims=True) + jnp.exp(sink - m)
        rows = slice(kh * HEAD_DIM, (kh + 1) * HEAD_DIM)
        vt = jnp.concatenate([vp_ref[rows, :], vc_ref[rows, :]], axis=1)
        o = _dot(vt, p.astype(BF16)) / denom
        for g, h in enumerate(heads):
            ot_ref[h * HEAD_DIM:(h + 1) * HEAD_DIM, :] = o[:, g * TQ:(g + 1) * TQ]
    o_ref[...] = ot_ref[...].T.astype(BF16)


def attn_prompt(sinks, qt, kb, vt):
    hq = qt.shape[0]
    kcur = pl.BlockSpec((TQ, KV_PLAIN), lambda i: (i, 0))
    kprev = pl.BlockSpec((TQ, KV_PLAIN), lambda i: (jnp.maximum(i - 1, 0), 0))
    vcur = pl.BlockSpec((KV_PLAIN, TQ), lambda i: (0, i))
    vprev = pl.BlockSpec((KV_PLAIN, TQ), lambda i: (0, jnp.maximum(i - 1, 0)))
    return pl.pallas_call(
        _attn_prompt_kernel,
        grid=(SEQ // TQ,),
        in_specs=[
            pl.BlockSpec(memory_space=pltpu.SMEM),
            pl.BlockSpec((hq, TQ), lambda i: (0, i)), kprev, kcur, vprev, vcur,
        ],
        out_specs=pl.BlockSpec((TQ, hq), lambda i: (i, 0)),
        out_shape=jax.ShapeDtypeStruct((SEQ, hq), BF16),
        scratch_shapes=[pltpu.VMEM((hq, TQ), F32)],
        compiler_params=_params(("parallel",)),
        name="attn_prompt",
    )(sinks, qt, kb, kb, vt, vt)


def _attend_group(q_ref, sink_ref, kh, kk, vv, bias, rows):
    lane = lax.broadcasted_iota(jnp.int32, (rows, LANES), 1)
    low = lane < HEAD_DIM
    qs, sc = [], []
    for pair in range(GROUP // 2):
        c0 = (kh * (GROUP // 2) + pair) * LANES
        qp = q_ref[:, c0:c0 + LANES]
        qs.append(jnp.where(low, qp, jnp.zeros_like(qp)))
        qs.append(jnp.where(low, jnp.zeros_like(qp), qp))
        for half in range(2):
            head = kh * GROUP + 2 * pair + half
            sc.append(jnp.full((rows, 1), sink_ref[head], F32))
    qs = jnp.concatenate(qs, axis=0)
    sink = jnp.concatenate(sc, axis=0)
    s = lax.dot_general(qs, kk, (((1,), (1,)), ((), ())), preferred_element_type=F32)
    s = s + bias
    m = jnp.maximum(jnp.max(s, axis=-1, keepdims=True), sink)
    p = jnp.exp(s - m)
    denom = jnp.sum(p, axis=-1, keepdims=True) + jnp.exp(sink - m)
    o = _dot(p.astype(BF16), vv) / denom
    outs = []
    for pair in range(GROUP // 2):
        o0 = o[(2 * pair) * rows:(2 * pair + 1) * rows]
        o1 = o[(2 * pair + 1) * rows:(2 * pair + 2) * rows]
        outs.append(jnp.where(low, o0, o1))
    return jnp.concatenate(outs, axis=1).astype(BF16)


KEYS_SAMPLE = 256


def _attn_sample_kernel(sink_ref, q_ref, k_ref, v_ref, o_ref):
    col = lax.broadcasted_iota(jnp.int32, (GROUP * DEC_SEQ, KEYS_SAMPLE), 1)
    bias = jnp.where(col < WINDOW + DEC_SEQ, 0.0, -jnp.inf).astype(F32)
    for kh in range(N_KV_HEADS):
        sl = slice(kh * LANES, (kh + 1) * LANES)
        o_ref[:, kh * GROUP * HEAD_DIM:(kh + 1) * GROUP * HEAD_DIM] = _attend_group(
            q_ref, sink_ref, kh, k_ref[0, :, sl], v_ref[0, :, sl], bias, DEC_SEQ)


def attn_sample(sinks, q, k2, v2):
    hq = q.shape[1]
    row = pl.BlockSpec((DEC_SEQ, hq), lambda b: (b, 0))
    keys = pl.BlockSpec((1, KEYS_SAMPLE, KV_DUP), lambda b: (b, 0, 0))
    return pl.pallas_call(
        _attn_sample_kernel,
        grid=(DEC_BATCH,),
        in_specs=[pl.BlockSpec(memory_space=pltpu.SMEM), row, keys, keys],
        out_specs=row,
        out_shape=jax.ShapeDtypeStruct((M_SAMPLE, hq), BF16),
        compiler_params=_params(("parallel",)),
        name="attn_sample",
    )(sinks, q, k2, v2)


def _rope_tables(pos):
    half = HEAD_DIM // 2
    inv_freq = ROPE_THETA ** (-jnp.arange(half, dtype=F32) / half)
    ang = pos.astype(F32)[:, None] * inv_freq[None, :]
    cos = jnp.cos(ang)
    sin = jnp.sin(ang)
    cos_l = jnp.tile(jnp.concatenate([cos, cos], axis=1), (1, LANES // HEAD_DIM))
    sin_l = jnp.tile(jnp.concatenate([-sin, sin], axis=1), (1, LANES // HEAD_DIM))
    return cos_l, sin_l, cos.T, sin.T


def _head_indicator(n):
    head = jnp.arange(n, dtype=jnp.int32)[:, None] // HEAD_DIM
    return (head == jnp.arange(LANES, dtype=jnp.int32)[None, :]).astype(BF16)


def _dup_heads(w):
    lead = w.shape[:-1]
    w4 = w.reshape(lead + (N_KV_HEADS, 1, HEAD_DIM))
    return jnp.broadcast_to(w4, lead + (N_KV_HEADS, 2, HEAD_DIM)).reshape(lead + (KV_DUP,))


def kernel(x_prompt, x_sample, state_conv, cache_k, cache_v, ffn_norm, w_gate, w_up, w_down,
           conv_norm, w_pw1, b_pw1, w_dw, b_dw, conv_ln_g, conv_ln_b, w_pw2, b_pw2,
           kv_norm, w_k, w_v, k_norm, attn_norm, w_q, q_norm, sinks, w_o):
    d = D_MODEL
    hq = N_HEADS * HEAD_DIM
    x = (x_prompt.reshape(SEQ, d), x_sample.reshape(M_SAMPLE, d), 0)

    pos = jnp.concatenate([jnp.arange(SEQ), jnp.tile(PAST_LEN + jnp.arange(DEC_SEQ), DEC_BATCH)])
    cos, sin, cos_t, sin_t = _rope_tables(pos)
    gsum = _head_indicator(KV_PLAIN)
    gexp = gsum.T
    zero_bias = jnp.zeros((1, d), F32)
    q_tile = 256

    conv_p, conv_s = [], []
    for i in range(DEPTH):
        if i < N_A:
            u = pw1_glu(*x, conv_norm[i][None], w_pw1, i, b_pw1[i][None])
            u_s = u[SEQ:].reshape(DEC_BATCH, DEC_SEQ, d)
            up_s = jnp.concatenate(
                [jnp.zeros((DEC_BATCH, HALO - CONV_PAD, d), F32), state_conv[i], u_s], axis=1)
            cargs = (w_dw[i], b_dw[i][None], conv_ln_g[i][None], conv_ln_b[i][None])
            cn_p = conv_prompt(u, *cargs)
            cn_s = conv_sample(up_s, *cargs).reshape(M_SAMPLE, d)
            y = proj_res(cn_p, cn_s, w_pw2, i, b_pw2[i][None], *x)
            conv_p.append(u[SEQ - CONV_PAD:SEQ][None])
            conv_s.append(up_s[:, -CONV_PAD:])
        else:
            j = i - N_A
            xs = x[0]
            gq = jnp.broadcast_to(q_norm[j][:, None], (HEAD_DIM, q_tile))
            qt = q_proj(xs, attn_norm[j][None], w_q, j, gq, cos_t, sin_t, tm=q_tile)
            o_p = attn_prompt(sinks[j], qt, kb, vt)
            o_s = attn_sample(sinks[j], qt[:, SEQ:].T, k2_s, v2_s)
            y = proj_res(o_p, o_s, w_o, j, zero_bias, *x)
        if i < DEPTH - 1:
            xs = ffn(y, ffn_norm[i][None], w_gate, w_up, w_down, i)
            x = (xs, xs, TAIL_BLOCK)
        else:
            y_p, y_s = ffn(y, ffn_norm[i][None], w_gate, w_up, w_down, i, split_out=True)
        if i == N_A - 1:
            w_kv = jnp.concatenate([w_k, w_v], axis=1).astype(BF16)
            gk = jnp.tile(k_norm, N_KV_HEADS)[None]
            kf, vf, kb, vt = kv_proj(xs, kv_norm[None], w_kv, gsum, gexp, gk, cos, sin)
            pad = jnp.zeros((DEC_BATCH, KEYS_SAMPLE - WINDOW - DEC_SEQ, KV_PLAIN), F32)
            dup_keys = lambda cache, new: _dup_heads(jnp.concatenate(
                [cache.reshape(DEC_BATCH, WINDOW, KV_PLAIN), new[SEQ:].reshape(DEC_BATCH, DEC_SEQ, KV_PLAIN), pad],
                axis=1)).astype(BF16)
            k2_s = dup_keys(cache_k, kf)
            v2_s = dup_keys(cache_v, vf)

    kv4 = lambda a, b: a.reshape(b, -1, N_KV_HEADS, HEAD_DIM)
    k_p = kv4(kf[SEQ - WINDOW:SEQ], 1)
    v_p = kv4(vf[SEQ - WINDOW:SEQ], 1)
    k_s = jnp.concatenate([cache_k, kv4(kf[SEQ:], DEC_BATCH)], axis=1)[:, -WINDOW:]
    v_s = jnp.concatenate([cache_v, kv4(vf[SEQ:], DEC_BATCH)], axis=1)[:, -WINDOW:]
    return (y_p.reshape(1, SEQ, d), y_s.reshape(DEC_BATCH, DEC_SEQ, d),
            jnp.stack(conv_p), k_p, v_p, jnp.stack(conv_s), k_s, v_s)
```
